```python
import jax, jax.numpy as jnp
from jax import lax
import numpy as np

D_MODEL = 4096
BATCH = 4
SEQ = 2048
DEPTH = 4
DEC_BATCH = 8
DEC_SEQ = 4
PAST_LEN = 8192
PAGE_SIZE = 128

N_A_LAYERS = DEPTH // 2
N_B_LAYERS = DEPTH - N_A_LAYERS
CONV_W = 3
N_HEADS = 32
HEAD_DIM = D_MODEL // N_HEADS
Q_BLOCK = 128
FORGET_BIAS_INIT = 4.0
N_GROUPS = 4
EXPERTS_PER_GROUP = 8
N_EXPERTS = N_GROUPS * EXPERTS_PER_GROUP
TOP_K = 2
D_EXPERT = D_MODEL // 4
PLE_DIM = 256
LN_EPS = 1e-5
ALPHA = (2 * DEPTH) ** 0.25
BETA = (8 * DEPTH) ** -0.25
ATTN_SCALE = HEAD_DIM ** -0.5

kernel_name = 'yoco_shortconv_fox_hmoe_decoder'

F32 = jnp.float32


def layer_norm(x, g, b):
    xf = x.astype(F32)
    mu = jnp.mean(xf, axis=-1, keepdims=True)
    xc = xf - mu
    var = jnp.mean(xc * xc, axis=-1, keepdims=True)
    return (xc * lax.rsqrt(var + LN_EPS) * g.astype(F32) + b.astype(F32)).astype(x.dtype)


def short_conv_mixer(x, w_in, w_dw, w_out, prev):
    bsz, seq, d = x.shape
    b_gate, c_gate, xt = jnp.split(x @ w_in, 3, axis=-1)
    u = c_gate * xt
    if prev is None:
        prev = jnp.zeros((bsz, CONV_W - 1, d), u.dtype)
    u_pad = jnp.concatenate([prev.astype(u.dtype), u], axis=1)
    conv = sum(w_dw[j] * u_pad[:, j:j + seq] for j in range(CONV_W))
    return (b_gate * conv) @ w_out, u_pad[:, seq:]


def shared_kv(x, kv_w, forget_w, forget_b):
    bsz, seq, _ = x.shape
    k, v = jnp.split(x @ kv_w, 2, axis=-1)
    k = k.reshape(bsz, seq, N_HEADS, HEAD_DIM)
    v = v.reshape(bsz, seq, N_HEADS, HEAD_DIM)
    logf = jax.nn.log_sigmoid((x @ forget_w).astype(F32) + forget_b.astype(F32))
    return k, v, logf


def fox_prompt(q, k, v, logf):
    bsz, seq = q.shape[:2]
    nb = seq // Q_BLOCK
    c = jnp.cumsum(logf, axis=1)
    c_key = c.transpose(0, 2, 1)[:, :, None, :]
    q_blocks = q.reshape(bsz, nb, Q_BLOCK, N_HEADS, HEAD_DIM).transpose(1, 0, 2, 3, 4)
    c_blocks = c.reshape(bsz, nb, Q_BLOCK, N_HEADS).transpose(1, 0, 3, 2)
    key_pos = jnp.arange(seq)

    def block(args):
        i, q_i, c_i = args
        s = jnp.einsum('bqhd,bkhd->bhqk', q_i, k).astype(F32) * ATTN_SCALE + (c_i[..., None] - c_key)
        q_pos = i * Q_BLOCK + jnp.arange(Q_BLOCK)
        s = jnp.where(key_pos[None, :] <= q_pos[:, None], s, -jnp.inf)
        p = jax.nn.softmax(s, axis=-1)
        return jnp.einsum('bhqk,bkhd->bqhd', p.astype(v.dtype), v)

    out = lax.map(block, (jnp.arange(nb), q_blocks, c_blocks))
    return out.transpose(1, 0, 2, 3, 4).reshape(bsz, seq, N_HEADS * HEAD_DIM)


def fox_with_past(q, k, v, logf, past_k, past_v, past_logf):
    bsz, seq = q.shape[:2]
    n_past = past_k.shape[1]
    c = jnp.cumsum(jnp.concatenate([past_logf.astype(F32), logf], axis=1), axis=1)
    c_past = c[:, :n_past].transpose(0, 2, 1)[:, :, None, :]
    c_new = c[:, n_past:].transpose(0, 2, 1)
    s_past = jnp.einsum('bqhd,bkhd->bhqk', q, past_k).astype(F32) * ATTN_SCALE + (c_new[..., None] - c_past)
    s_new = jnp.einsum('bqhd,bkhd->bhqk', q, k).astype(F32) * ATTN_SCALE + (c_new[..., None] - c_new[:, :, None, :])
    causal = jnp.tril(jnp.ones((seq, seq), bool))
    s_new = jnp.where(causal, s_new, -jnp.inf)
    p = jax.nn.softmax(jnp.concatenate([s_past, s_new], axis=-1), axis=-1)
    out = (jnp.einsum('bhqk,bkhd->bqhd', p[..., :n_past].astype(past_v.dtype), past_v)
           + jnp.einsum('bhqk,bkhd->bqhd', p[..., n_past:].astype(v.dtype), v))
    return out.reshape(bsz, seq, N_HEADS * HEAD_DIM)


def moe_dispatch(h2, experts, gates, w_gu, w_down):
    t, d = h2.shape
    n_assign = t * TOP_K
    blk = max(8, min(128, n_assign // N_EXPERTS))
    n_blocks = (n_assign + N_EXPERTS * (blk - 1)) // blk + 1
    e_flat = experts.reshape(n_assign)
    order = jnp.argsort(e_flat)
    e_sorted = e_flat[order]
    tok_sorted = order // TOP_K
    gate_sorted = gates.reshape(n_assign)[order]
    counts = jnp.zeros((N_EXPERTS,), jnp.int32).at[e_flat].add(1)
    padded = (counts + blk - 1) // blk * blk
    pad_end = jnp.cumsum(padded)
    pad_start = pad_end - padded
    start = jnp.cumsum(counts) - counts
    dest = pad_start[e_sorted] + jnp.arange(n_assign, dtype=jnp.int32) - start[e_sorted]
    slot_tok = jnp.full((n_blocks * blk,), t, jnp.int32).at[dest].set(tok_sorted)
    block_expert = jnp.minimum(
        jnp.searchsorted(pad_end, jnp.arange(n_blocks, dtype=jnp.int32) * blk, side='right'), N_EXPERTS - 1)
    xs = jnp.concatenate([h2, jnp.zeros((1, d), h2.dtype)], axis=0)[slot_tok].reshape(n_blocks, blk, d)

    def expert_block(args):
        xb, e = args
        g, u = jnp.split(xb @ w_gu[e], 2, axis=-1)
        return (jax.nn.silu(g) * u) @ w_down[e]

    ys = lax.map(expert_block, (xs, block_expert)).reshape(n_blocks * blk, d)
    contrib = ys[dest] * gate_sorted[:, None].astype(ys.dtype)
    return jnp.zeros((t, d), ys.dtype).at[tok_sorted].add(contrib)


def hier_moe(h, rg_w, rg_b, re_w, re_b, w_gu, w_down):
    bsz, seq, d = h.shape
    t = bsz * seq
    h2 = h.reshape(t, d)
    lg = (h2 @ rg_w).astype(F32) + rg_b.astype(F32)
    grp = jnp.argmax(lg, axis=-1)
    p_grp = jnp.take_along_axis(jax.nn.softmax(lg, axis=-1), grp[:, None], axis=-1)
    le = ((h2 @ re_w).astype(F32) + re_b.astype(F32)).reshape(t, N_GROUPS, EXPERTS_PER_GROUP)
    le = jnp.take_along_axis(le, grp[:, None, None], axis=1)[:, 0]
    top_p, top_i = lax.top_k(jax.nn.softmax(le, axis=-1), TOP_K)
    gates = p_grp * top_p / jnp.sum(top_p, axis=-1, keepdims=True)
    experts = (grp[:, None] * EXPERTS_PER_GROUP + top_i).astype(jnp.int32)
    return moe_dispatch(h2, experts, gates, w_gu, w_down).reshape(bsz, seq, d)


def per_layer_embed(h, p_i, w_proj, w_gate):
    return (p_i.astype(h.dtype) @ w_proj) * jax.nn.sigmoid(h @ w_gate)


def trunk(x, p, conv_state, past_k, past_v, past_logf,
          conv_w_in, conv_w_dw, conv_w_out, attn_w_q, attn_w_o, kv_w, forget_w, forget_b,
          router_g_w, router_g_b, router_e_w, router_e_b, moe_w_gu, moe_w_down,
          ln_g, ln_b, ple_w_proj, ple_w_gate):
    bsz, seq, _ = x.shape
    new_conv = []
    for i in range(DEPTH):
        if i < N_A_LAYERS:
            mix, st = short_conv_mixer(x, conv_w_in[i], conv_w_dw[i], conv_w_out[i],
                                       None if conv_state is None else conv_state[i])
            new_conv.append(st)
        else:
            if i == N_A_LAYERS:
                k_new, v_new, logf_new = shared_kv(x, kv_w, forget_w, forget_b)
            j = i - N_A_LAYERS
            q = (x @ attn_w_q[j]).reshape(bsz, seq, N_HEADS, HEAD_DIM)
            if past_k is None:
                att = fox_prompt(q, k_new, v_new, logf_new)
            else:
                att = fox_with_past(q, k_new, v_new, logf_new, past_k, past_v, past_logf)
            mix = att @ attn_w_o[j]
        h = layer_norm(ALPHA * x + mix, ln_g[i, 0], ln_b[i, 0])
        moe_out = hier_moe(h, router_g_w[i], router_g_b[i], router_e_w[i], router_e_b[i],
                           moe_w_gu[i], moe_w_down[i])
        h = layer_norm(ALPHA * h + moe_out, ln_g[i, 1], ln_b[i, 1])
        ple = per_layer_embed(h, p[i], ple_w_proj[i], ple_w_gate[i])
        x = layer_norm(ALPHA * h + ple, ln_g[i, 2], ln_b[i, 2])
    return x, jnp.stack(new_conv), k_new, v_new, logf_new


def setup_inputs(seed: int = 0) -> dict:
    key = jax.random.key(seed)
    ks = iter(jax.random.split(key, 32))
    d = D_MODEL

    def nrm(shape, scale):
        return jax.random.normal(next(ks), shape, F32) * scale

    n_pages = PAST_LEN // PAGE_SIZE
    n_used = DEC_BATCH * n_pages
    n_phys = n_used + max(1, n_used // 4)
    perm = jax.random.permutation(next(ks), n_phys)
    page_table = perm[:n_used].reshape(DEC_BATCH, n_pages).astype(jnp.int32)
    return {
        'x_prompt': nrm((BATCH, SEQ, d), 1.0),
        'x_sample': nrm((DEC_BATCH, DEC_SEQ, d), 1.0),
        'state_conv': nrm((N_A_LAYERS, DEC_BATCH, CONV_W - 1, d), 1.0),
        'cache_k': nrm((n_phys, PAGE_SIZE, N_HEADS, HEAD_DIM), 1.0),
        'cache_v': nrm((n_phys, PAGE_SIZE, N_HEADS, HEAD_DIM), BETA),
        'cache_logf': jax.nn.log_sigmoid(FORGET_BIAS_INIT + nrm((n_phys, PAGE_SIZE, N_HEADS), 1.0)),
        'page_table': page_table,
        'p_prompt': nrm((DEPTH, BATCH, SEQ, PLE_DIM), 1.0),
        'p_sample': nrm((DEPTH, DEC_BATCH, DEC_SEQ, PLE_DIM), 1.0),
        'conv_w_in': nrm((N_A_LAYERS, d, 3 * d), d ** -0.5),
        'conv_w_dw': nrm((N_A_LAYERS, CONV_W, d), CONV_W ** -0.5),
        'conv_w_out': nrm((N_A_LAYERS, d, d), BETA * d ** -0.5),
        'attn_w_q': nrm((N_B_LAYERS, d, d), d ** -0.5),
        'attn_w_o': nrm((N_B_LAYERS, d, d), BETA * d ** -0.5),
        'kv_w': jnp.concatenate([nrm((d, d), d ** -0.5), nrm((d, d), BETA * d ** -0.5)], axis=1),
        'forget_w': nrm((d, N_HEADS), d ** -0.5),
        'forget_b': FORGET_BIAS_INIT + nrm((N_HEADS,), 0.1),
        'router_g_w': nrm((DEPTH, d, N_GROUPS), d ** -0.5),
        'router_g_b': nrm((DEPTH, N_GROUPS), 0.01),
        'router_e_w': nrm((DEPTH, d, N_EXPERTS), d ** -0.5),
        'router_e_b': nrm((DEPTH, N_EXPERTS), 0.01),
        'moe_w_gu': nrm((DEPTH, N_EXPERTS, d, 2 * D_EXPERT), d ** -0.5),
        'moe_w_down': nrm((DEPTH, N_EXPERTS, D_EXPERT, d), BETA * D_EXPERT ** -0.5),
        'ln_g': 1.0 + nrm((DEPTH, 3, d), 0.02),
        'ln_b': nrm((DEPTH, 3, d), 0.02),
        'ple_w_proj': nrm((DEPTH, PLE_DIM, d), BETA * PLE_DIM ** -0.5),
        'ple_w_gate': nrm((DEPTH, d, d), d ** -0.5),
    }


def reference(x_prompt, x_sample, state_conv, cache_k, cache_v, cache_logf, page_table, p_prompt, p_sample,
              conv_w_in, conv_w_dw, conv_w_out, attn_w_q, attn_w_o, kv_w, forget_w, forget_b,
              router_g_w, router_g_b, router_e_w, router_e_b, moe_w_gu, moe_w_down,
              ln_g, ln_b, ple_w_proj, ple_w_gate):
    weights = (conv_w_in, conv_w_dw, conv_w_out, attn_w_q, attn_w_o, kv_w, forget_w, forget_b,
               router_g_w, router_g_b, router_e_w, router_e_b, moe_w_gu, moe_w_down,
               ln_g, ln_b, ple_w_proj, ple_w_gate)
    y_prompt, conv_prompt, k_prompt, v_prompt, logf_prompt = trunk(
        x_prompt, p_prompt, None, None, None, None, *weights)
    dec_b, n_pages = page_table.shape
    past_len = n_pages * cache_k.shape[1]
    past_k = cache_k[page_table].reshape(dec_b, past_len, N_HEADS, HEAD_DIM)
    past_v = cache_v[page_table].reshape(dec_b, past_len, N_HEADS, HEAD_DIM)
    past_logf = cache_logf[page_table].reshape(dec_b, past_len, N_HEADS)
    y_sample, conv_sample, k_sample, v_sample, logf_sample = trunk(
        x_sample, p_sample, state_conv, past_k, past_v, past_logf, *weights)
    return (y_prompt, y_sample, conv_prompt, k_prompt, v_prompt, logf_prompt,
            conv_sample, k_sample, v_sample, logf_sample)
```

```python
import functools

import jax
import jax.numpy as jnp
from jax import lax
from jax.experimental import pallas as pl
from jax.experimental.pallas import tpu as pltpu

F32 = jnp.float32
BF16 = jnp.bfloat16
I32 = jnp.int32
HIGHEST = lax.Precision.HIGHEST

LN_EPS = 1e-5
TOP_K = 2
V7X_VMEM_LIMIT_BYTES = 56 * 1024 * 1024
LANES = 128
SUBLANES = 8

_NT = (((1,), (1,)), ((), ()))
_TN = (((0,), (0,)), ((), ()))


def _params(semantics, vmem=V7X_VMEM_LIMIT_BYTES):
    return pltpu.CompilerParams(dimension_semantics=semantics, vmem_limit_bytes=vmem)


def _tile(n, pref):
    t = min(n, pref)
    while n % t:
        t //= 2
    return t


def _dot(a, b):
    return jnp.dot(a, b, preferred_element_type=F32)


def _dot_hi(a, b):
    return jnp.dot(a, b, preferred_element_type=F32, precision=HIGHEST)


def _cast_weights(w_refs, wbf_refs):
    @pl.when(pl.program_id(1) == 0)
    def _():
        for w, wb in zip(w_refs, wbf_refs):
            wb[...] = w[...].astype(BF16)


def _mm_plain_kernel(x_ref, w_ref, *rest, n_out):
    o_refs, wbf = rest[:n_out], rest[n_out]
    _cast_weights([w_ref], [wbf])
    acc = _dot(x_ref[...], wbf[...])
    for o in o_refs:
        o[...] = acc.astype(o.dtype)


def _mm_convin_kernel(x_ref, wb_ref, wc_ref, wx_ref, b_out, u_out, sb, sc, sx):
    _cast_weights([wb_ref, wc_ref, wx_ref], [sb, sc, sx])
    x = x_ref[...]
    b_out[...] = _dot(x, sb[...])
    u_out[...] = _dot(x, sc[...]) * _dot(x, sx[...])


def _mm_ple_kernel(x_ref, w_ref, p_ref, wp_ref, o_ref, wbf):
    _cast_weights([w_ref], [wbf])
    gate = _dot(x_ref[...], wbf[...])
    proj = _dot(p_ref[...].astype(BF16), wp_ref[...].astype(BF16))
    o_ref[...] = proj * (1.0 / (1.0 + jnp.exp(-gate)))


def _w_spec(w, lead, k, tn, col_off):
    none = (None,) * len(lead)
    return pl.BlockSpec(none + (k, tn), lambda n, m: lead + (0, col_off + n))


def _dense(kern, x, weights, out_dtypes, *, n_cols, tm, tn, extra=(), extra_specs=(), name):
    m_rows, k = x.shape
    tm = _tile(m_rows, tm)
    tn = _tile(n_cols, tn)
    grid = (n_cols // tn, m_rows // tm)
    in_specs = [pl.BlockSpec((tm, k), lambda n, m: (m, 0))]
    args = [x]
    for w, lead, off in weights:
        in_specs.append(_w_spec(w, lead, k, tn, off // tn))
        args.append(w)
    in_specs += list(extra_specs)
    args += list(extra)
    out_shape = [jax.ShapeDtypeStruct((m_rows, n_cols), dt) for dt in out_dtypes]
    out_specs = [pl.BlockSpec((tm, tn), lambda n, m: (m, n)) for _ in out_dtypes]
    scratch = [pltpu.VMEM((k, tn), BF16) for _ in weights]
    return pl.pallas_call(
        kern, grid=grid, in_specs=in_specs, out_specs=out_specs, out_shape=out_shape,
        scratch_shapes=scratch, compiler_params=_params(("arbitrary", "arbitrary")), name=name,
    )(*args)


def _mm(x, w, lead, col_off, out_dtypes, name, n_cols):
    kern = functools.partial(_mm_plain_kernel, n_out=len(out_dtypes))
    return _dense(kern, x, [(w, lead, col_off)], out_dtypes, n_cols=n_cols, tm=1024, tn=512, name=name)


def _mm_convin(x, w_in, layer, d):
    ws = [(w_in, (layer,), j * d) for j in range(3)]
    return _dense(_mm_convin_kernel, x, ws, [F32, F32], n_cols=d, tm=512, tn=256, name="conv_in_proj")


def _mm_ple(h_bf, w_gate, p, w_proj, layer, d):
    m_rows = h_bf.shape[0]
    tm, tn = _tile(m_rows, 1024), _tile(d, 512)
    pd = p.shape[-1]
    extra_specs = [pl.BlockSpec((tm, pd), lambda n, m: (m, 0)),
                   pl.BlockSpec((None, pd, tn), lambda n, m: (layer, 0, n))]
    return _dense(_mm_ple_kernel, h_bf, [(w_gate, (layer,), 0)], [F32], n_cols=d, tm=tm, tn=tn,
                  extra=(p, w_proj), extra_specs=extra_specs, name="ple_gate_proj")[0]


def _forget_kernel(x_ref, w_ref, b_ref, o_ref):
    z = _dot(x_ref[...], w_ref[...].astype(BF16)) + b_ref[...]
    o_ref[...] = -(jnp.maximum(-z, 0.0) + jnp.log1p(jnp.exp(-jnp.abs(z))))


def _forget(x_bf, forget_w, forget_b):
    m_rows, k = x_bf.shape
    h = forget_w.shape[1]
    tm = _tile(m_rows, 1024)
    return pl.pallas_call(
        _forget_kernel, grid=(m_rows // tm,),
        in_specs=[pl.BlockSpec((tm, k), lambda m: (m, 0)),
                  pl.BlockSpec((k, h), lambda m: (0, 0)),
                  pl.BlockSpec((1, h), lambda m: (0, 0))],
        out_specs=pl.BlockSpec((tm, h), lambda m: (m, 0)),
        out_shape=jax.ShapeDtypeStruct((m_rows, h), F32),
        compiler_params=_params(("arbitrary",)), name="forget_proj",
    )(x_bf, forget_w, forget_b.reshape(1, h))


def _ln_rows(v, g, b):
    mu = jnp.mean(v, axis=-1, keepdims=True)
    xc = v - mu
    var = jnp.mean(xc * xc, axis=-1, keepdims=True)
    return xc * lax.rsqrt(var + LN_EPS) * g + b


def _ln_kernel(x_ref, mix_ref, g_ref, b_ref, of_ref, ob_ref, *, alpha):
    h = _ln_rows(alpha * x_ref[...] + mix_ref[...], g_ref[...], b_ref[...])
    of_ref[...] = h
    ob_ref[...] = h.astype(BF16)


def _ln_moe_kernel(x_ref, y_ref, gt_ref, g_ref, b_ref, of_ref, ob_ref, *, alpha, d):
    gt = gt_ref[...]
    moe = y_ref[:, :d] * gt[:, 0:1] + y_ref[:, d:] * gt[:, 1:2]
    h = _ln_rows(alpha * x_ref[...] + moe, g_ref[...], b_ref[...])
    of_ref[...] = h
    ob_ref[...] = h.astype(BF16)


def _ln_router_kernel(x_ref, mix_ref, g_ref, b_ref, wr_ref, br_ref, of_ref, e_ref, gate_ref,
                      *, alpha, n_groups, per_group):
    h = _ln_rows(alpha * x_ref[...] + mix_ref[...], g_ref[...], b_ref[...])
    of_ref[...] = h
    logits = lax.dot_general(wr_ref[...], h, _NT, precision=HIGHEST,
                             preferred_element_type=F32) + br_ref[...]
    tm = logits.shape[1]
    lg = logits[0:n_groups]
    gmax = jnp.max(lg, axis=0, keepdims=True)
    gi = lax.broadcasted_iota(I32, lg.shape, 0)
    grp = jnp.min(jnp.where(lg == gmax, gi, n_groups), axis=0, keepdims=True)
    p_grp = 1.0 / jnp.sum(jnp.exp(lg - gmax), axis=0, keepdims=True)
    sel = logits[SUBLANES + (n_groups - 1) * per_group:SUBLANES + n_groups * per_group]
    for g in range(n_groups - 2, -1, -1):
        sel = jnp.where(grp == g, logits[SUBLANES + g * per_group:SUBLANES + (g + 1) * per_group], sel)
    ex = jnp.exp(sel - jnp.max(sel, axis=0, keepdims=True))
    ei = lax.broadcasted_iota(I32, (per_group, tm), 0)
    p1 = jnp.max(ex, axis=0, keepdims=True)
    i1 = jnp.min(jnp.where(ex == p1, ei, per_group), axis=0, keepdims=True)
    ex2 = jnp.where(ei == i1, -1.0, ex)
    p2 = jnp.max(ex2, axis=0, keepdims=True)
    i2 = jnp.min(jnp.where(ex2 == p2, ei, per_group), axis=0, keepdims=True)
    scale = p_grp / (p1 + p2)
    e_ref[0:1, :] = grp * per_group + i1
    e_ref[1:2, :] = grp * per_group + i2
    gate_ref[0:1, :] = p1 * scale
    gate_ref[1:2, :] = p2 * scale


def _row_spec(tm, width, row_off_blocks=0):
    return pl.BlockSpec((tm, width), lambda i: (row_off_blocks + i, 0))


def _vec_spec(d, layer, which):
    return pl.BlockSpec((None, None, 1, d), lambda i: (layer, which, 0, 0))


def _ln(x, mix, ln_g4, ln_b4, layer, which, alpha, n_rows, x_row_off=0, tm_pref=256):
    d = mix.shape[1]
    tm = _tile(n_rows, tm_pref)
    return pl.pallas_call(
        functools.partial(_ln_kernel, alpha=alpha), grid=(n_rows // tm,),
        in_specs=[_row_spec(tm, d, x_row_off // tm), _row_spec(tm, d),
                  _vec_spec(d, layer, which), _vec_spec(d, layer, which)],
        out_specs=[_row_spec(tm, d), _row_spec(tm, d)],
        out_shape=[jax.ShapeDtypeStruct((n_rows, d), F32), jax.ShapeDtypeStruct((n_rows, d), BF16)],
        compiler_params=_params(("arbitrary",)), name="ln_residual",
    )(x, mix, ln_g4, ln_b4)


def _ln_moe(h_all, ys2v, gates_t, ln_g4, ln_b4, layer, alpha, n_rows, row_off, tm_pref=256):
    d = h_all.shape[1]
    tm = _tile(n_rows, tm_pref)
    off = row_off // tm
    return pl.pallas_call(
        functools.partial(_ln_moe_kernel, alpha=alpha, d=d), grid=(n_rows // tm,),
        in_specs=[_row_spec(tm, d, off), _row_spec(tm, 2 * d, off), _row_spec(tm, TOP_K, off),
                  _vec_spec(d, layer, 1), _vec_spec(d, layer, 1)],
        out_specs=[_row_spec(tm, d), _row_spec(tm, d)],
        out_shape=[jax.ShapeDtypeStruct((n_rows, d), F32), jax.ShapeDtypeStruct((n_rows, d), BF16)],
        compiler_params=_params(("arbitrary",)), name="ln_moe_combine",
    )(h_all, ys2v, gates_t, ln_g4, ln_b4)


def _ln_router(x, mix, ln_g4, ln_b4, wr_t, br, layer, alpha, n_groups, per_group, tm_pref=256):
    n_rows, d = x.shape
    tm = _tile(n_rows, tm_pref)
    r = wr_t.shape[1]
    return pl.pallas_call(
        functools.partial(_ln_router_kernel, alpha=alpha, n_groups=n_groups, per_group=per_group),
        grid=(n_rows // tm,),
        in_specs=[_row_spec(tm, d), _row_spec(tm, d), _vec_spec(d, layer, 0), _vec_spec(d, layer, 0),
                  pl.BlockSpec((None, r, d), lambda i: (layer, 0, 0)),
                  pl.BlockSpec((None, r, 1), lambda i: (layer, 0, 0))],
        out_specs=[_row_spec(tm, d), pl.BlockSpec((TOP_K, tm), lambda i: (0, i)),
                   pl.BlockSpec((TOP_K, tm), lambda i: (0, i))],
        out_shape=[jax.ShapeDtypeStruct((n_rows, d), F32), jax.ShapeDtypeStruct((TOP_K, n_rows), I32),
                   jax.ShapeDtypeStruct((TOP_K, n_rows), F32)],
        compiler_params=_params(("arbitrary",)), name="ln_router",
    )(x, mix, ln_g4, ln_b4, wr_t, br)


def _conv_kernel(b_ref, u_ref, halo_ref, w_ref, o_ref, scr, *, ts):
    base = SUBLANES
    scr[base - 2:base, :] = halo_ref[...]
    scr[base:base + ts, :] = u_ref[...]
    w = w_ref[...]
    conv = (w[0:1] * scr[base - 2:base - 2 + ts, :] + w[1:2] * scr[base - 1:base - 1 + ts, :]
            + w[2:3] * scr[base:base + ts, :])
    o_ref[...] = (b_ref[...] * conv).astype(o_ref.dtype)


def _short_conv(b, u, prev, w_dw, layer, seq, out_dtype):
    t, d = u.shape
    ts = _tile(seq, 256)
    n_tiles = t // ts
    per_seq = seq // ts
    u3 = u.reshape(n_tiles, ts, d)
    if per_seq > 1:
        tails = u3[:, ts - 2:, :]
        shifted = jnp.concatenate([jnp.zeros((1, 2, d), F32), tails[:-1]], axis=0)
        first = (jnp.arange(n_tiles) % per_seq == 0)[:, None, None]
        halo = jnp.where(first, jnp.repeat(prev, per_seq, axis=0), shifted)
    else:
        halo = prev
    blk = pl.BlockSpec((None, ts, d), lambda i: (i, 0, 0))
    out = pl.pallas_call(
        functools.partial(_conv_kernel, ts=ts), grid=(n_tiles,),
        in_specs=[blk, blk, pl.BlockSpec((None, 2, d), lambda i: (i, 0, 0)),
                  pl.BlockSpec((None, 3, d), lambda i: (layer, 0, 0))],
        out_specs=blk, out_shape=jax.ShapeDtypeStruct((n_tiles, ts, d), out_dtype),
        scratch_shapes=[pltpu.VMEM((ts + SUBLANES, d), F32)],
        compiler_params=_params(("arbitrary",)), name="short_conv",
    )(b.reshape(n_tiles, ts, d), u3, halo, w_dw)
    return out.reshape(t, d)


def _cumsum_kernel(lf_ref, lft_ref, c_ref, ct_ref, *, blk):
    s, h = lf_ref.shape
    r = lax.broadcasted_iota(I32, (blk, blk), 0)
    c = lax.broadcasted_iota(I32, (blk, blk), 1)
    lower = (c <= r).astype(F32)
    upper = (r <= c).astype(F32)
    carry = jnp.zeros((1, h), F32)
    carry_t = jnp.zeros((h, 1), F32)
    for j in range(s // blk):
        x = lf_ref[j * blk:(j + 1) * blk, :]
        c_ref[j * blk:(j + 1) * blk, :] = _dot_hi(lower, x) + carry
        carry = carry + jnp.sum(x, axis=0, keepdims=True)
        xt = lft_ref[:, j * blk:(j + 1) * blk]
        ct_ref[:, j * blk:(j + 1) * blk] = _dot_hi(xt, upper) + carry_t
        carry_t = carry_t + jnp.sum(xt, axis=1, keepdims=True)


def _logf_cumsum(logf, bsz, seq):
    h = logf.shape[1]
    lf = logf.reshape(bsz, seq, h)
    lft = lf.transpose(0, 2, 1)
    blk = _tile(seq, 256)
    c, ct = pl.pallas_call(
        functools.partial(_cumsum_kernel, blk=blk), grid=(bsz,),
        in_specs=[pl.BlockSpec((None, seq, h), lambda b: (b, 0, 0)),
                  pl.BlockSpec((None, h, seq), lambda b: (b, 0, 0))],
        out_specs=[pl.BlockSpec((None, seq, h), lambda b: (b, 0, 0)),
                   pl.BlockSpec((None, h, seq), lambda b: (b, 0, 0))],
        out_shape=[jax.ShapeDtypeStruct((bsz, seq, h), F32), jax.ShapeDtypeStruct((bsz, h, seq), F32)],
        compiler_params=_params(("arbitrary",)), name="logf_cumsum",
    )(lf, lft)
    return c, ct.reshape(bsz, h, 1, seq)


def _flash_kernel(q_ref, k_ref, v_ref, c_ref, ct_ref, o_ref, *, tq, tk, scale):
    hd = pl.program_id(1)
    qi = pl.program_id(2)
    q = q_ref[...]
    cblk = c_ref[...]
    lane = lax.broadcasted_iota(I32, cblk.shape, 1)
    cq = jnp.sum(jnp.where(lane == hd, cblk, 0.0), axis=1, keepdims=True)
    row = lax.broadcasted_iota(I32, (tq, tk), 0) + qi * tq
    col = lax.broadcasted_iota(I32, (tq, tk), 1)

    def body(j, carry):
        m, l, acc = carry
        start = pl.multiple_of(j * tk, tk)
        kj = k_ref[pl.ds(start, tk), :]
        vj = v_ref[pl.ds(start, tk), :]
        ck = ct_ref[:, pl.ds(start, tk)]
        s = lax.dot_general(q, kj, _NT, preferred_element_type=F32) * scale + (cq - ck)
        s = jnp.where(col + start <= row, s, -jnp.inf)
        m_new = jnp.maximum(m, jnp.max(s, axis=1, keepdims=True))
        a = jnp.exp(m - m_new)
        p = jnp.exp(s - m_new)
        l = a * l + jnp.sum(p, axis=1, keepdims=True)
        acc = a * acc + _dot(p.astype(BF16), vj)
        return m_new, l, acc

    n_kv = (qi * tq + tq + tk - 1) // tk
    init = (jnp.full((tq, 1), -jnp.inf, F32), jnp.zeros((tq, 1), F32), jnp.zeros((tq, q.shape[1]), F32))
    m, l, acc = lax.fori_loop(0, n_kv, body, init)
    o_ref[...] = (acc / l).astype(o_ref.dtype)


def _fox_prompt(q, k, v, c, ct, bsz, seq, n_heads, dh):
    t, d = q.shape
    tq = _tile(seq, 512)
    tk = tq
    nq = seq // tq
    return pl.pallas_call(
        functools.partial(_flash_kernel, tq=tq, tk=tk, scale=dh ** -0.5),
        grid=(bsz, n_heads, nq),
        in_specs=[pl.BlockSpec((tq, dh), lambda b, h, i: (b * nq + i, h)),
                  pl.BlockSpec((seq, dh), lambda b, h, i: (b, h)),
                  pl.BlockSpec((seq, dh), lambda b, h, i: (b, h)),
                  pl.BlockSpec((None, tq, n_heads), lambda b, h, i: (b, i, 0)),
                  pl.BlockSpec((None, None, 1, seq), lambda b, h, i: (b, h, 0, 0))],
        out_specs=pl.BlockSpec((tq, dh), lambda b, h, i: (b * nq + i, h)),
        out_shape=jax.ShapeDtypeStruct((t, d), BF16),
        compiler_params=_params(("arbitrary", "arbitrary", "arbitrary")), name="fox_prompt",
    )(q, k, v, c, ct)


def _head_match(rows, cols, n_heads):
    r = lax.broadcasted_iota(I32, (rows, cols), 0) & (n_heads - 1)
    c = lax.broadcasted_iota(I32, (rows, cols), 1) & (n_heads - 1)
    return r == c


def _decode_scores_kernel(pt_ref, k_ref, lf_ref, qbd_ref, cn_ref, s_ref, carry, *, n_heads, scale):
    p = pl.program_id(1)
    page, h = lf_ref.shape
    qh = qbd_ref.shape[1]

    @pl.when(p == 0)
    def _():
        carry[...] = cn_ref[...]

    expand = _head_match(h, qh, n_heads).astype(F32)
    lft = _dot_hi(lf_ref[...], expand)
    r = lax.broadcasted_iota(I32, (page, page), 0)
    c = lax.broadcasted_iota(I32, (page, page), 1)
    later = (c > r).astype(F32)
    bias = _dot_hi(later, lft) + carry[...]
    s = _dot(k_ref[...].astype(BF16), qbd_ref[...]) * scale + bias
    s_ref[...] = s
    carry[...] = carry[...] + jnp.sum(lft, axis=0, keepdims=True)


def _new_bias(ln_row, n_heads, n_new):
    qh = ln_row.shape[1]
    shift = n_heads.bit_length() - 1
    jrow = lax.broadcasted_iota(I32, (SUBLANES, qh), 0)
    ipos = lax.broadcasted_iota(I32, (SUBLANES, qh), 1) >> shift
    lmask = jnp.where(ipos <= jrow, jnp.broadcast_to(ln_row, (SUBLANES, qh)), 0.0)
    same = _head_match(qh, qh, n_heads).astype(F32)
    tcum = _dot_hi(lmask, same)
    cn = jnp.sum(jnp.where(jrow == ipos, tcum, 0.0), axis=0, keepdims=True)
    valid = (jrow <= ipos) & (jrow < n_new)
    return cn, jnp.where(valid, cn - tcum, -jnp.inf)


def _decode_cn_kernel(ln_ref, cn_ref, *, n_heads, n_new):
    cn, _ = _new_bias(ln_ref[...], n_heads, n_new)
    cn_ref[...] = cn


def _decode_pv_kernel(pt_ref, s_ref, v_ref, qbd_ref, kn_ref, vn_ref, ln_ref, o_ref,
                      m_scr, linv_scr, pn_scr, acc, *, n_heads, n_new, scale, dh):
    p = pl.program_id(1)
    n_pages = pl.num_programs(1)
    page = v_ref.shape[0]
    qh = qbd_ref.shape[1]

    @pl.when(p == 0)
    def _():
        _, bias_new = _new_bias(ln_ref[...], n_heads, n_new)
        s_new = _dot(kn_ref[...], qbd_ref[...]) * scale + bias_new
        s_all = s_ref[...]
        m = jnp.maximum(jnp.max(s_all, axis=0, keepdims=True), jnp.max(s_new, axis=0, keepdims=True))
        e_new = jnp.exp(s_new - m)
        l = jnp.sum(jnp.exp(s_all - m), axis=0, keepdims=True) + jnp.sum(e_new, axis=0, keepdims=True)
        linv = 1.0 / l
        m_scr[...] = m
        linv_scr[...] = linv
        pn_scr[...] = e_new * linv
        acc[...] = jnp.zeros_like(acc)

    start = pl.multiple_of(p * page, page)
    ps = jnp.exp(s_ref[pl.ds(start, page), :] - m_scr[...]) * linv_scr[...]
    acc[...] += lax.dot_general(ps.astype(BF16), v_ref[...].astype(BF16), _TN, preferred_element_type=F32)

    @pl.when(p == n_pages - 1)
    def _():
        total = acc[...] + lax.dot_general(pn_scr[...].astype(BF16), vn_ref[...], _TN,
                                           preferred_element_type=F32)
        d = total.shape[1]
        shift = dh.bit_length() - 1
        rh = lax.broadcasted_iota(I32, (qh, d), 0) & (n_heads - 1)
        ch = lax.broadcasted_iota(I32, (qh, d), 1) >> shift
        own = jnp.where(rh == ch, total, 0.0)
        o_ref[...] = jnp.sum(own.reshape(n_new, n_heads, d), axis=1)


def _fox_decode(q_bf, k_new_bf, v_new_bf, logf_new, cache_k3, cache_v3, cache_logf, page_table,
                n_heads, dh):
    bd, n_pages = page_table.shape
    n_new = q_bf.shape[0] // bd
    d = q_bf.shape[1]
    page = cache_k3.shape[1]
    qh = n_new * n_heads
    past = n_pages * page
    scale = dh ** -0.5
    pt_flat = page_table.reshape(-1)
    q4 = q_bf.reshape(bd, n_new, n_heads, dh).transpose(0, 2, 3, 1)
    eye = jnp.eye(n_heads, dtype=bool)[None, :, None, None, :]
    qbd = jnp.where(eye, q4[..., None], jnp.zeros((), BF16)).reshape(bd, d, qh)
    ln_row = logf_new.reshape(bd, 1, qh)
    pad = jnp.zeros((bd, SUBLANES - n_new, d), BF16)
    kn8 = jnp.concatenate([k_new_bf.reshape(bd, n_new, d), pad], axis=1)
    vn8 = jnp.concatenate([v_new_bf.reshape(bd, n_new, d), pad], axis=1)

    cn = pl.pallas_call(
        functools.partial(_decode_cn_kernel, n_heads=n_heads, n_new=n_new), grid=(bd,),
        in_specs=[pl.BlockSpec((None, 1, qh), lambda b: (b, 0, 0))],
        out_specs=pl.BlockSpec((None, 1, qh), lambda b: (b, 0, 0)),
        out_shape=jax.ShapeDtypeStruct((bd, 1, qh), F32),
        compiler_params=_params(("arbitrary",)), name="fox_decode_newcum",
    )(ln_row)

    rev = lambda b, p, pt: (pt[b * n_pages + (n_pages - 1 - p)], 0, 0)
    scores = pl.pallas_call(
        functools.partial(_decode_scores_kernel, n_heads=n_heads, scale=scale),
        grid_spec=pltpu.PrefetchScalarGridSpec(
            num_scalar_prefetch=1, grid=(bd, n_pages),
            in_specs=[pl.BlockSpec((None, page, d), rev),
                      pl.BlockSpec((None, page, n_heads), rev),
                      pl.BlockSpec((None, d, qh), lambda b, p, pt: (b, 0, 0)),
                      pl.BlockSpec((None, 1, qh), lambda b, p, pt: (b, 0, 0))],
            out_specs=pl.BlockSpec((None, page, qh), lambda b, p, pt: (b, n_pages - 1 - p, 0)),
            scratch_shapes=[pltpu.VMEM((1, qh), F32)]),
        out_shape=jax.ShapeDtypeStruct((bd, past, qh), F32),
        compiler_params=_params(("arbitrary", "arbitrary")), name="fox_decode_scores",
    )(pt_flat, cache_k3, cache_logf, qbd, cn)

    fwd = lambda b, p, pt: (pt[b * n_pages + p], 0, 0)
    per_b = lambda b, p, pt: (b, 0, 0)
    out = pl.pallas_call(
        functools.partial(_decode_pv_kernel, n_heads=n_heads, n_new=n_new, scale=scale, dh=dh),
        grid_spec=pltpu.PrefetchScalarGridSpec(
            num_scalar_prefetch=1, grid=(bd, n_pages),
            in_specs=[pl.BlockSpec((None, past, qh), per_b),
                      pl.BlockSpec((None, page, d), fwd),
                      pl.BlockSpec((None, d, qh), per_b),
                      pl.BlockSpec((None, SUBLANES, d), per_b),
                      pl.BlockSpec((None, SUBLANES, d), per_b),
                      pl.BlockSpec((None, 1, qh), per_b)],
            out_specs=pl.BlockSpec((None, n_new, d), per_b),
            scratch_shapes=[pltpu.VMEM((1, qh), F32), pltpu.VMEM((1, qh), F32),
                            pltpu.VMEM((SUBLANES, qh), F32), pltpu.VMEM((qh, d), F32)]),
        out_shape=jax.ShapeDtypeStruct((bd, n_new, d), F32),
        compiler_params=_params(("arbitrary", "arbitrary")), name="fox_decode_pv",
    )(pt_flat, scores, cache_v3, qbd, kn8, vn8, ln_row)
    return out.reshape(bd * n_new, d)


def _row_gather_kernel(idx_ref, src_ref, dst_ref, sem, *, n_rows, chunk):
    def issue(c):
        def body(r, carry):
            i = c * chunk + r
            pltpu.make_async_copy(src_ref.at[pl.ds(idx_ref[i], 1)], dst_ref.at[pl.ds(i, 1)], sem).start()
            return carry
        lax.fori_loop(0, chunk, body, 0)

    def wait(c):
        pltpu.make_async_copy(src_ref.at[pl.ds(0, chunk)], dst_ref.at[pl.ds(c * chunk, chunk)], sem).wait()

    n_chunks = n_rows // chunk
    issue(0)

    def loop(c, carry):
        issue(c)
        wait(c - 1)
        return carry
    lax.fori_loop(1, n_chunks, loop, 0)
    wait(n_chunks - 1)


def _row_gather(src, idx, name):
    n_rows = idx.shape[0]
    chunk = _tile(n_rows, 256)
    return pl.pallas_call(
        functools.partial(_row_gather_kernel, n_rows=n_rows, chunk=chunk),
        grid_spec=pltpu.PrefetchScalarGridSpec(
            num_scalar_prefetch=1, grid=(),
            in_specs=[pl.BlockSpec(memory_space=pl.ANY)],
            out_specs=pl.BlockSpec(memory_space=pl.ANY),
            scratch_shapes=[pltpu.SemaphoreType.DMA]),
        out_shape=jax.ShapeDtypeStruct((n_rows, src.shape[1]), src.dtype),
        name=name,
    )(idx, src)


def _moe_kernel(be_ref, nu_ref, xs_ref, wg_ref, wu_ref, wd_ref, o_ref, xbf):
    b = pl.program_id(0)
    f = pl.program_id(1)

    @pl.when(b < nu_ref[0])
    def _():
        @pl.when(f == 0)
        def _():
            xbf[...] = xs_ref[...].astype(BF16)
        x = xbf[...]
        g = _dot(x, wg_ref[...].astype(BF16))
        u = _dot(x, wu_ref[...].astype(BF16))
        hmid = (g / (1.0 + jnp.exp(-g))) * u
        y = _dot(hmid.astype(BF16), wd_ref[...].astype(BF16))

        @pl.when(f == 0)
        def _():
            o_ref[...] = y

        @pl.when(f > 0)
        def _():
            o_ref[...] += y


def _moe_experts(xs, w_gu, w_down, layer, block_expert, n_used, tm):
    n_slots, d = xs.shape
    fdim = w_down.shape[2]
    tf = _tile(fdim, 256)
    n_f = fdim // tf
    n_blocks = n_slots // tm

    def blk(b, nu):
        return jnp.minimum(b, nu[0] - 1)

    def feff(b, f, nu):
        return jnp.where(b < nu[0], f, n_f - 1)

    return pl.pallas_call(
        _moe_kernel,
        grid_spec=pltpu.PrefetchScalarGridSpec(
            num_scalar_prefetch=2, grid=(n_blocks, n_f),
            in_specs=[pl.BlockSpec((tm, d), lambda b, f, be, nu: (blk(b, nu), 0)),
                      pl.BlockSpec((None, None, d, tf),
                                   lambda b, f, be, nu: (layer, be[blk(b, nu)], 0, feff(b, f, nu))),
                      pl.BlockSpec((None, None, d, tf),
                                   lambda b, f, be, nu: (layer, be[blk(b, nu)], 0, n_f + feff(b, f, nu))),
                      pl.BlockSpec((None, None, tf, d),
                                   lambda b, f, be, nu: (layer, be[blk(b, nu)], feff(b, f, nu), 0))],
            out_specs=pl.BlockSpec((tm, d), lambda b, f, be, nu: (blk(b, nu), 0)),
            scratch_shapes=[pltpu.VMEM((tm, d), BF16)]),
        out_shape=jax.ShapeDtypeStruct((n_slots, d), F32),
        compiler_params=_params(("arbitrary", "arbitrary")), name="moe_experts",
    )(block_expert, n_used, xs, w_gu, w_gu, w_down)


def _dispatch(experts, n_experts, tm):
    t = experts.shape[1]
    n_assign = TOP_K * t
    n_blocks = -(-n_assign // tm) + n_experts
    n_slots = n_blocks * tm
    e_flat = experts.reshape(n_assign)
    onehot = (e_flat[:, None] == jnp.arange(n_experts, dtype=I32)[None, :]).astype(I32)
    rank = jnp.sum((jnp.cumsum(onehot, axis=0) - onehot) * onehot, axis=1)
    counts = jnp.sum(onehot, axis=0)
    padded = (counts + tm - 1) // tm * tm
    pad_end = jnp.cumsum(padded)
    pad_start = pad_end - padded
    dest = pad_start[e_flat] + rank
    tok = jnp.arange(n_assign, dtype=I32) % t
    slot_tok = jnp.zeros((n_slots,), I32).at[dest].set(tok)
    n_used = (pad_end[-1] // tm).astype(I32)
    blocks = jnp.arange(n_blocks, dtype=I32)
    block_expert = jnp.minimum(jnp.searchsorted(pad_end, jnp.minimum(blocks, n_used - 1) * tm, side="right"),
                               n_experts - 1).astype(I32)
    unsort = dest.reshape(TOP_K, t).T.reshape(n_assign)
    return slot_tok, unsort, block_expert, n_used.reshape(1)


def kernel(x_prompt, x_sample, state_conv, cache_k, cache_v, cache_logf, page_table, p_prompt, p_sample,
           conv_w_in, conv_w_dw, conv_w_out, attn_w_q, attn_w_o, kv_w, forget_w, forget_b,
           router_g_w, router_g_b, router_e_w, router_e_b, moe_w_gu, moe_w_down,
           ln_g, ln_b, ple_w_proj, ple_w_gate):
    bsz, seq, d = x_prompt.shape
    bd, dseq, _ = x_sample.shape
    depth = ln_g.shape[0]
    n_a = conv_w_in.shape[0]
    n_heads, dh = cache_k.shape[2], cache_k.shape[3]
    n_groups = router_g_w.shape[-1]
    n_experts = router_e_w.shape[-1]
    per_group = n_experts // n_groups
    alpha = (2 * depth) ** 0.25
    tp, ts = bsz * seq, bd * dseq
    t_all = tp + ts
    moe_tm = 256
    assert n_heads & (n_heads - 1) == 0 and dh & (dh - 1) == 0 and n_groups <= SUBLANES
    assert tp % ts == 0 and per_group == SUBLANES

    ln_g4 = ln_g.reshape(depth, 3, 1, d)
    ln_b4 = ln_b.reshape(depth, 3, 1, d)
    zpad = jnp.zeros((depth, SUBLANES - n_groups, d), F32)
    wr_t = jnp.concatenate([router_g_w.transpose(0, 2, 1), zpad, router_e_w.transpose(0, 2, 1)], axis=1)
    br = jnp.concatenate([router_g_b, jnp.zeros((depth, SUBLANES - n_groups), F32), router_e_b],
                         axis=1)[:, :, None]
    cache_k3 = cache_k.reshape(cache_k.shape[0], cache_k.shape[1], d)
    cache_v3 = cache_v.reshape(cache_v.shape[0], cache_v.shape[1], d)

    groups = [
        dict(x=x_prompt.reshape(tp, d), p=p_prompt.reshape(depth, tp, -1), n=tp, seq=seq, nseq=bsz,
             prev=None, conv=[]),
        dict(x=x_sample.reshape(ts, d), p=p_sample.reshape(depth, ts, -1), n=ts, seq=dseq, nseq=bd,
             prev=state_conv, conv=[]),
    ]
    for g in groups:
        g["xb"] = g["x"].astype(BF16)

    for i in range(depth):
        for gi, g in enumerate(groups):
            if i < n_a:
                bgate, u = _mm_convin(g["xb"], conv_w_in, i, d)
                prev = jnp.zeros((g["nseq"], 2, d), F32) if g["prev"] is None else g["prev"][i]
                gated = _short_conv(bgate, u, prev, conv_w_dw, i, g["seq"], BF16 if gi == 0 else F32)
                g["conv"].append(u.reshape(g["nseq"], g["seq"], d)[:, g["seq"] - 2:, :])
                g["mix"] = _mm(gated.astype(BF16), conv_w_out, (i,), 0, [F32], "conv_out_proj", d)[0]
            else:
                if i == n_a:
                    g["k"], g["kb"] = _mm(g["xb"], kv_w, (), 0, [F32, BF16], "k_proj", d)
                    g["v"], g["vb"] = _mm(g["xb"], kv_w, (), d, [F32, BF16], "v_proj", d)
                    g["logf"] = _forget(g["xb"], forget_w, forget_b)
                    if gi == 0:
                        g["c"], g["ct"] = _logf_cumsum(g["logf"], bsz, seq)
                j = i - n_a
                q = _mm(g["xb"], attn_w_q, (j,), 0, [BF16], "q_proj", d)[0]
                if gi == 0:
                    att = _fox_prompt(q, g["kb"], g["vb"], g["c"], g["ct"], bsz, seq, n_heads, dh)
                else:
                    att = _fox_decode(q, g["kb"], g["vb"], g["logf"], cache_k3, cache_v3, cache_logf,
                                      page_table, n_heads, dh).astype(BF16)
                g["mix"] = _mm(att, attn_w_o, (j,), 0, [F32], "attn_out_proj", d)[0]
        hs, es, gs = [], [], []
        for g in groups:
            h, e, gate = _ln_router(g["x"], g["mix"], ln_g4, ln_b4, wr_t, br, i, alpha, n_groups, per_group)
            hs.append(h)
            es.append(e)
            gs.append(gate)
        h_all = jnp.concatenate(hs, axis=0)
        experts = jnp.concatenate(es, axis=1)
        gates_t = jnp.concatenate(gs, axis=1).T
        slot_tok, unsort, block_expert, n_used = _dispatch(experts, n_experts, moe_tm)
        xs = _row_gather(h_all, slot_tok, "moe_gather")
        ys = _moe_experts(xs, moe_w_gu, moe_w_down, i, block_expert, n_used, moe_tm)
        ys2v = _row_gather(ys, unsort, "moe_unsort").reshape(t_all, TOP_K * d)
        row_off = 0
        for g in groups:
            h2, h2b = _ln_moe(h_all, ys2v, gates_t, ln_g4, ln_b4, i, alpha, g["n"], row_off)
            row_off += g["n"]
            ple = _mm_ple(h2b, ple_w_gate, g["p"][i], ple_w_proj, i, d)
            g["x"], g["xb"] = _ln(h2, ple, ln_g4, ln_b4, i, 2, alpha, g["n"])

    gp, gs_ = groups
    return (gp["x"].reshape(bsz, seq, d), gs_["x"].reshape(bd, dseq, d),
            jnp.stack(gp["conv"]),
            gp["k"].reshape(bsz, seq, n_heads, dh), gp["v"].reshape(bsz, seq, n_heads, dh),
            gp["logf"].reshape(bsz, seq, n_heads),
            jnp.stack(gs_["conv"]),
            gs_["k"].reshape(bd, dseq, n_heads, dh), gs_["v"].reshape(bd, dseq, n_heads, dh),
            gs_["logf"].reshape(bd, dseq, n_heads))
```

```python
import functools

import jax
import jax.numpy as jnp
from jax import lax
from jax.experimental import pallas as pl
from jax.experimental.pallas import tpu as pltpu

F32 = jnp.float32
BF16 = jnp.bfloat16
I32 = jnp.int32
HIGHEST = lax.Precision.HIGHEST

LN_EPS = 1e-5
LOG2E = 1.4426950408889634
TOP_K = 2
V7X_VMEM_LIMIT_BYTES = 56 * 1024 * 1024
SUBLANES = 8

_NT = (((1,), (1,)), ((), ()))


def _params(semantics, vmem=V7X_VMEM_LIMIT_BYTES):
    return pltpu.CompilerParams(dimension_semantics=semantics, vmem_limit_bytes=vmem)


def _tile(n, pref):
    t = min(n, pref)
    while n % t:
        t //= 2
    return t


def _dot(a, b):
    return jnp.dot(a, b, preferred_element_type=F32)


def _dot_hi(a, b):
    return jnp.dot(a, b, preferred_element_type=F32, precision=HIGHEST)


def _cast_weights(w_refs, wbf_refs):
    @pl.when(pl.program_id(1) == 0)
    def _():
        for w, wb in zip(w_refs, wbf_refs):
            wb[...] = w[...].astype(BF16)


def _mm_plain_kernel(x_ref, w_ref, *rest, n_out):
    o_refs, wbf = rest[:n_out], rest[n_out]
    _cast_weights([w_ref], [wbf])
    acc = _dot(x_ref[...], wbf[...])
    for o in o_refs:
        o[...] = acc.astype(o.dtype)


def _mm_convin_kernel(x_ref, wb_ref, wc_ref, wx_ref, b_out, u_out, sb, sc, sx):
    _cast_weights([wb_ref, wc_ref, wx_ref], [sb, sc, sx])
    x = x_ref[...]
    b_out[...] = _dot(x, sb[...])
    u_out[...] = _dot(x, sc[...]) * _dot(x, sx[...])


def _mm_ple_kernel(x_ref, w_ref, p_ref, wp_ref, o_ref, wbf):
    _cast_weights([w_ref], [wbf])
    gate = _dot(x_ref[...], wbf[...])
    proj = _dot(p_ref[...].astype(BF16), wp_ref[...].astype(BF16))
    o_ref[...] = proj * (1.0 / (1.0 + jnp.exp(-gate)))


def _w_spec(w, lead, k, tn, col_off):
    none = (None,) * len(lead)
    return pl.BlockSpec(none + (k, tn), lambda n, m: lead + (0, col_off + n))


def _dense(kern, x, weights, out_dtypes, *, n_cols, tm, tn, extra=(), extra_specs=(), name):
    m_rows, k = x.shape
    tm = _tile(m_rows, tm)
    tn = _tile(n_cols, tn)
    grid = (n_cols // tn, m_rows // tm)
    in_specs = [pl.BlockSpec((tm, k), lambda n, m: (m, 0))]
    args = [x]
    for w, lead, off in weights:
        in_specs.append(_w_spec(w, lead, k, tn, off // tn))
        args.append(w)
    in_specs += list(extra_specs)
    args += list(extra)
    out_shape = [jax.ShapeDtypeStruct((m_rows, n_cols), dt) for dt in out_dtypes]
    out_specs = [pl.BlockSpec((tm, tn), lambda n, m: (m, n)) for _ in out_dtypes]
    scratch = [pltpu.VMEM((k, tn), BF16) for _ in weights]
    return pl.pallas_call(
        kern, grid=grid, in_specs=in_specs, out_specs=out_specs, out_shape=out_shape,
        scratch_shapes=scratch, compiler_params=_params(("arbitrary", "arbitrary")), name=name,
    )(*args)


def _mm(x, w, lead, col_off, out_dtypes, name, n_cols):
    kern = functools.partial(_mm_plain_kernel, n_out=len(out_dtypes))
    return _dense(kern, x, [(w, lead, col_off)], out_dtypes, n_cols=n_cols, tm=1024, tn=512, name=name)


def _mm_convin(x, w_in, layer, d):
    ws = [(w_in, (layer,), j * d) for j in range(3)]
    return _dense(_mm_convin_kernel, x, ws, [F32, F32], n_cols=d, tm=512, tn=256, name="conv_in_proj")


def _mm_ple(h_bf, w_gate, p, w_proj, layer, d):
    m_rows = h_bf.shape[0]
    tm, tn = _tile(m_rows, 1024), _tile(d, 512)
    pd = p.shape[-1]
    extra_specs = [pl.BlockSpec((tm, pd), lambda n, m: (m, 0)),
                   pl.BlockSpec((None, pd, tn), lambda n, m: (layer, 0, n))]
    return _dense(_mm_ple_kernel, h_bf, [(w_gate, (layer,), 0)], [F32], n_cols=d, tm=tm, tn=tn,
                  extra=(p, w_proj), extra_specs=extra_specs, name="ple_gate_proj")[0]


def _forget_kernel(x_ref, w_ref, b_ref, o_ref):
    z = _dot(x_ref[...], w_ref[...].astype(BF16)) + b_ref[...]
    o_ref[...] = -(jnp.maximum(-z, 0.0) + jnp.log1p(jnp.exp(-jnp.abs(z))))


def _forget(x_bf, forget_w, forget_b):
    m_rows, k = x_bf.shape
    h = forget_w.shape[1]
    tm = _tile(m_rows, 1024)
    return pl.pallas_call(
        _forget_kernel, grid=(m_rows // tm,),
        in_specs=[pl.BlockSpec((tm, k), lambda m: (m, 0)),
                  pl.BlockSpec((k, h), lambda m: (0, 0)),
                  pl.BlockSpec((1, h), lambda m: (0, 0))],
        out_specs=pl.BlockSpec((tm, h), lambda m: (m, 0)),
        out_shape=jax.ShapeDtypeStruct((m_rows, h), F32),
        compiler_params=_params(("arbitrary",)), name="forget_proj",
    )(x_bf, forget_w, forget_b.reshape(1, h))


def _heads_layout_kernel(x_ref, o_ref, *, n_heads, dh):
    for h in range(n_heads):
        o_ref[:, h, :] = x_ref[:, h * dh:(h + 1) * dh]


def _heads_layout(x, n_heads, dh):
    t, d = x.shape
    tm = _tile(t, 256)
    return pl.pallas_call(
        functools.partial(_heads_layout_kernel, n_heads=n_heads, dh=dh), grid=(t // tm,),
        in_specs=[pl.BlockSpec((tm, d), lambda i: (i, 0))],
        out_specs=pl.BlockSpec((tm, n_heads, dh), lambda i: (i, 0, 0)),
        out_shape=jax.ShapeDtypeStruct((t, n_heads, dh), x.dtype),
        compiler_params=_params(("arbitrary",)), name="heads_layout",
    )(x)


def _ln_rows(v, g, b):
    mu = jnp.mean(v, axis=-1, keepdims=True)
    xc = v - mu
    var = jnp.mean(xc * xc, axis=-1, keepdims=True)
    return xc * lax.rsqrt(var + LN_EPS) * g + b


def _ln_kernel(x_ref, mix_ref, g_ref, b_ref, of_ref, ob_ref, *, alpha):
    h = _ln_rows(alpha * x_ref[...] + mix_ref[...], g_ref[...], b_ref[...])
    of_ref[...] = h
    ob_ref[...] = h.astype(BF16)


def _ln_moe_kernel(dest_ref, x_ref, ys_ref, gt_ref, g_ref, b_ref, of_ref, ob_ref, buf, sem,
                   *, alpha, tm, row_off, t_all):
    i = pl.program_id(0)
    slot = i % 2

    def issue(step, into):
        base = row_off + step * tm

        def body(r4, carry):
            for j in range(4):
                r = r4 * 4 + j
                for k in range(TOP_K):
                    pltpu.make_async_copy(ys_ref.at[pl.ds(dest_ref[k * t_all + base + r], 1)],
                                          buf.at[into, k, pl.ds(r, 1)], sem.at[into]).start()
            return carry
        lax.fori_loop(0, tm // 4, body, 0)

    @pl.when(i == 0)
    def _():
        issue(0, 0)

    @pl.when(i + 1 < pl.num_programs(0))
    def _():
        issue(i + 1, 1 - slot)
    for k in range(TOP_K):
        pltpu.make_async_copy(ys_ref.at[pl.ds(0, tm)], buf.at[slot, k], sem.at[slot]).wait()
    gt = gt_ref[...]
    moe = buf[slot, 0] * gt[:, 0:1] + buf[slot, 1] * gt[:, 1:2]
    h = _ln_rows(alpha * x_ref[...] + moe, g_ref[...], b_ref[...])
    of_ref[...] = h
    ob_ref[...] = h.astype(BF16)


def _ln_router_kernel(x_ref, mix_ref, g_ref, b_ref, wr_ref, br_ref, of_ref, e_ref, gate_ref,
                      *, alpha, n_groups, per_group):
    h = _ln_rows(alpha * x_ref[...] + mix_ref[...], g_ref[...], b_ref[...])
    of_ref[...] = h
    logits = lax.dot_general(wr_ref[...], h, _NT, precision=HIGHEST,
                             preferred_element_type=F32) + br_ref[...]
    tm = logits.shape[1]
    lg = logits[0:n_groups]
    gmax = jnp.max(lg, axis=0, keepdims=True)
    gi = lax.broadcasted_iota(I32, lg.shape, 0)
    grp = jnp.min(jnp.where(lg == gmax, gi, n_groups), axis=0, keepdims=True)
    p_grp = 1.0 / jnp.sum(jnp.exp(lg - gmax), axis=0, keepdims=True)
    sel = logits[SUBLANES + (n_groups - 1) * per_group:SUBLANES + n_groups * per_group]
    for g in range(n_groups - 2, -1, -1):
        sel = jnp.where(grp == g, logits[SUBLANES + g * per_group:SUBLANES + (g + 1) * per_group], sel)
    ex = jnp.exp(sel - jnp.max(sel, axis=0, keepdims=True))
    ei = lax.broadcasted_iota(I32, (per_group, tm), 0)
    p1 = jnp.max(ex, axis=0, keepdims=True)
    i1 = jnp.min(jnp.where(ex == p1, ei, per_group), axis=0, keepdims=True)
    ex2 = jnp.where(ei == i1, -1.0, ex)
    p2 = jnp.max(ex2, axis=0, keepdims=True)
    i2 = jnp.min(jnp.where(ex2 == p2, ei, per_group), axis=0, keepdims=True)
    scale = p_grp / (p1 + p2)
    e_ref[0:1, :] = grp * per_group + i1
    e_ref[1:2, :] = grp * per_group + i2
    gate_ref[0:1, :] = p1 * scale
    gate_ref[1:2, :] = p2 * scale


def _row_spec(tm, width, row_off_blocks=0):
    return pl.BlockSpec((tm, width), lambda i: (row_off_blocks + i, 0))


def _vec_spec(d, layer, which):
    return pl.BlockSpec((None, None, 1, d), lambda i: (layer, which, 0, 0))


def _ln(x, mix, ln_g4, ln_b4, layer, which, alpha, n_rows, x_row_off=0, tm_pref=256):
    d = mix.shape[1]
    tm = _tile(n_rows, tm_pref)
    return pl.pallas_call(
        functools.partial(_ln_kernel, alpha=alpha), grid=(n_rows // tm,),
        in_specs=[_row_spec(tm, d, x_row_off // tm), _row_spec(tm, d),
                  _vec_spec(d, layer, which), _vec_spec(d, layer, which)],
        out_specs=[_row_spec(tm, d), _row_spec(tm, d)],
        out_shape=[jax.ShapeDtypeStruct((n_rows, d), F32), jax.ShapeDtypeStruct((n_rows, d), BF16)],
        compiler_params=_params(("arbitrary",)), name="ln_residual",
    )(x, mix, ln_g4, ln_b4)


def _ln_moe(h_all, ys, dest, gates_t, ln_g4, ln_b4, layer, alpha, n_rows, row_off, tm_pref=256):
    t_all, d = h_all.shape
    tm = _tile(n_rows, tm_pref)
    off = row_off // tm
    rows = lambda w: pl.BlockSpec((tm, w), lambda i, dst: (off + i, 0))
    vec = pl.BlockSpec((None, None, 1, d), lambda i, dst: (layer, 1, 0, 0))
    out = pl.BlockSpec((tm, d), lambda i, dst: (i, 0))
    return pl.pallas_call(
        functools.partial(_ln_moe_kernel, alpha=alpha, tm=tm, row_off=row_off, t_all=t_all),
        grid_spec=pltpu.PrefetchScalarGridSpec(
            num_scalar_prefetch=1, grid=(n_rows // tm,),
            in_specs=[rows(d), pl.BlockSpec(memory_space=pl.ANY), rows(TOP_K), vec, vec],
            out_specs=[out, out],
            scratch_shapes=[pltpu.VMEM((2, TOP_K, tm, d), F32), pltpu.SemaphoreType.DMA((2,))]),
        out_shape=[jax.ShapeDtypeStruct((n_rows, d), F32), jax.ShapeDtypeStruct((n_rows, d), BF16)],
        compiler_params=_params(("arbitrary",)), name="ln_moe_combine",
    )(dest, h_all, ys, gates_t, ln_g4, ln_b4)


def _ln_router(x, mix, ln_g4, ln_b4, wr_t, br, layer, alpha, n_groups, per_group, tm_pref=256):
    n_rows, d = x.shape
    tm = _tile(n_rows, tm_pref)
    r = wr_t.shape[1]
    return pl.pallas_call(
        functools.partial(_ln_router_kernel, alpha=alpha, n_groups=n_groups, per_group=per_group),
        grid=(n_rows // tm,),
        in_specs=[_row_spec(tm, d), _row_spec(tm, d), _vec_spec(d, layer, 0), _vec_spec(d, layer, 0),
                  pl.BlockSpec((None, r, d), lambda i: (layer, 0, 0)),
                  pl.BlockSpec((None, r, 1), lambda i: (layer, 0, 0))],
        out_specs=[_row_spec(tm, d), pl.BlockSpec((TOP_K, tm), lambda i: (0, i)),
                   pl.BlockSpec((TOP_K, tm), lambda i: (0, i))],
        out_shape=[jax.ShapeDtypeStruct((n_rows, d), F32), jax.ShapeDtypeStruct((TOP_K, n_rows), I32),
                   jax.ShapeDtypeStruct((TOP_K, n_rows), F32)],
        compiler_params=_params(("arbitrary",)), name="ln_router",
    )(x, mix, ln_g4, ln_b4, wr_t, br)


def _conv_kernel(b_ref, u_ref, halo_ref, w_ref, o_ref, scr, *, ts):
    base = SUBLANES
    scr[base - 2:base, :] = halo_ref[...]
    scr[base:base + ts, :] = u_ref[...]
    w = w_ref[...]
    conv = (w[0:1] * scr[base - 2:base - 2 + ts, :] + w[1:2] * scr[base - 1:base - 1 + ts, :]
            + w[2:3] * scr[base:base + ts, :])
    o_ref[...] = (b_ref[...] * conv).astype(o_ref.dtype)


def _short_conv(b, u, prev, w_dw, layer, seq, out_dtype):
    t, d = u.shape
    ts = _tile(seq, 256)
    n_tiles = t // ts
    per_seq = seq // ts
    u3 = u.reshape(n_tiles, ts, d)
    if per_seq > 1:
        tails = u3[:, ts - 2:, :]
        shifted = jnp.concatenate([jnp.zeros((1, 2, d), F32), tails[:-1]], axis=0)
        first = (jnp.arange(n_tiles) % per_seq == 0)[:, None, None]
        halo = jnp.where(first, jnp.repeat(prev, per_seq, axis=0), shifted)
    else:
        halo = prev
    blk = pl.BlockSpec((None, ts, d), lambda i: (i, 0, 0))
    out = pl.pallas_call(
        functools.partial(_conv_kernel, ts=ts), grid=(n_tiles,),
        in_specs=[blk, blk, pl.BlockSpec((None, 2, d), lambda i: (i, 0, 0)),
                  pl.BlockSpec((None, 3, d), lambda i: (layer, 0, 0))],
        out_specs=blk, out_shape=jax.ShapeDtypeStruct((n_tiles, ts, d), out_dtype),
        scratch_shapes=[pltpu.VMEM((ts + SUBLANES, d), F32)],
        compiler_params=_params(("arbitrary",)), name="short_conv",
    )(b.reshape(n_tiles, ts, d), u3, halo, w_dw)
    return out.reshape(t, d)


def _cumsum_kernel(lf_ref, lft_ref, c_ref, ct_ref, *, blk):
    s, h = lf_ref.shape
    r = lax.broadcasted_iota(I32, (blk, blk), 0)
    c = lax.broadcasted_iota(I32, (blk, blk), 1)
    lower = (c <= r).astype(F32)
    upper = (r <= c).astype(F32)
    carry = jnp.zeros((1, h), F32)
    carry_t = jnp.zeros((h, 1), F32)
    for j in range(s // blk):
        x = lf_ref[j * blk:(j + 1) * blk, :]
        c_ref[j * blk:(j + 1) * blk, :] = _dot_hi(lower, x) + carry
        carry = carry + jnp.sum(x, axis=0, keepdims=True)
        xt = lft_ref[:, j * blk:(j + 1) * blk]
        ct_ref[:, j * blk:(j + 1) * blk] = _dot_hi(xt, upper) + carry_t
        carry_t = carry_t + jnp.sum(xt, axis=1, keepdims=True)


def _logf_cumsum(logf, bsz, seq):
    h = logf.shape[1]
    lf = logf.reshape(bsz, seq, h)
    lft = lf.transpose(0, 2, 1)
    blk = _tile(seq, 256)
    c, ct = pl.pallas_call(
        functools.partial(_cumsum_kernel, blk=blk), grid=(bsz,),
        in_specs=[pl.BlockSpec((None, seq, h), lambda b: (b, 0, 0)),
                  pl.BlockSpec((None, h, seq), lambda b: (b, 0, 0))],
        out_specs=[pl.BlockSpec((None, seq, h), lambda b: (b, 0, 0)),
                   pl.BlockSpec((None, h, seq), lambda b: (b, 0, 0))],
        out_shape=[jax.ShapeDtypeStruct((bsz, seq, h), F32), jax.ShapeDtypeStruct((bsz, h, seq), F32)],
        compiler_params=_params(("arbitrary",)), name="logf_cumsum",
    )(lf, lft)
    return c, ct.reshape(bsz, h, 1, seq)


def _flash_kernel(q_ref, k_ref, v_ref, c_ref, ct_ref, o_ref, *, tq, tk, scale):
    hd = pl.program_id(1)
    qi = pl.program_id(2)
    q = q_ref[...]
    cblk = c_ref[...]
    lane = lax.broadcasted_iota(I32, cblk.shape, 1)
    cq2 = jnp.sum(jnp.where(lane == hd, cblk, 0.0), axis=1, keepdims=True) * LOG2E
    scale2 = scale * LOG2E

    def tile(j, carry, diagonal):
        m, l, acc = carry
        start = pl.multiple_of(j * tk, tk)
        kj = k_ref[pl.ds(start, tk), :]
        vj = v_ref[pl.ds(start, tk), :]
        ck2 = ct_ref[:, pl.ds(start, tk)] * LOG2E
        s = lax.dot_general(q, kj, _NT, preferred_element_type=F32) * scale2 + (cq2 - ck2)
        if diagonal:
            row = lax.broadcasted_iota(I32, (tq, tk), 0)
            col = lax.broadcasted_iota(I32, (tq, tk), 1)
            s = jnp.where(col <= row, s, -jnp.inf)
        m_new = jnp.maximum(m, jnp.max(s, axis=1, keepdims=True))
        a = jnp.exp2(m - m_new)
        p = jnp.exp2(s - m_new)
        l = a * l + jnp.sum(p, axis=1, keepdims=True)
        acc = a * acc + _dot(p.astype(BF16), vj)
        return m_new, l, acc

    init = (jnp.full((tq, 1), -jnp.inf, F32), jnp.zeros((tq, 1), F32), jnp.zeros((tq, q.shape[1]), F32))
    carry = lax.fori_loop(0, qi, lambda j, c: tile(j, c, False), init)
    m, l, acc = tile(qi, carry, True)
    o_ref[...] = (acc / l).astype(o_ref.dtype)


def _fox_prompt(q, k, v, c, ct, bsz, seq, n_heads, dh):
    t, d = q.shape
    tq = _tile(seq, 512)
    tk = tq
    nq = seq // tq
    return pl.pallas_call(
        functools.partial(_flash_kernel, tq=tq, tk=tk, scale=dh ** -0.5),
        grid=(bsz, n_heads, nq),
        in_specs=[pl.BlockSpec((tq, dh), lambda b, h, i: (b * nq + i, h)),
                  pl.BlockSpec((seq, dh), lambda b, h, i: (b, h)),
                  pl.BlockSpec((seq, dh), lambda b, h, i: (b, h)),
                  pl.BlockSpec((None, tq, n_heads), lambda b, h, i: (b, i, 0)),
                  pl.BlockSpec((None, None, 1, seq), lambda b, h, i: (b, h, 0, 0))],
        out_specs=pl.BlockSpec((tq, dh), lambda b, h, i: (b * nq + i, h)),
        out_shape=jax.ShapeDtypeStruct((t, d), BF16),
        compiler_params=_params(("arbitrary", "arbitrary", "arbitrary")), name="fox_prompt",
    )(q, k, v, c, ct)


def _head_match(rows, cols, n_heads):
    r = lax.broadcasted_iota(I32, (rows, cols), 0) & (n_heads - 1)
    c = lax.broadcasted_iota(I32, (rows, cols), 1) & (n_heads - 1)
    return r == c


def _page_scores(k3, qall, n_heads):
    p, h, dh = k3.shape
    qh = qall.shape[1]
    s2 = _dot(k3.reshape(p * h, dh).astype(BF16), qall)
    own = _head_match(h, qh, n_heads)
    return jnp.sum(jnp.where(own[None], s2.reshape(p, h, qh), 0.0), axis=1)


def _page_pv(ps, v3, rep, msk):
    p, h, dh = v3.shape
    pexp = _dot(ps.T.astype(BF16), rep).astype(BF16) * msk
    return _dot(pexp, v3.reshape(p * h, dh).astype(BF16))


def _decode_scores_kernel(pt_ref, k_ref, lf_ref, qall_ref, cn_ref, s_ref, carry, *, n_heads, scale):
    p = pl.program_id(1)
    page, h = lf_ref.shape
    qh = qall_ref.shape[1]

    @pl.when(p == 0)
    def _():
        carry[...] = cn_ref[...]

    expand = _head_match(h, qh, n_heads).astype(F32)
    lft = _dot_hi(lf_ref[...], expand)
    r = lax.broadcasted_iota(I32, (page, page), 0)
    c = lax.broadcasted_iota(I32, (page, page), 1)
    later = (c > r).astype(F32)
    bias = _dot_hi(later, lft) + carry[...]
    s_ref[...] = _page_scores(k_ref[...], qall_ref[...], n_heads) * scale + bias
    carry[...] = carry[...] + jnp.sum(lft, axis=0, keepdims=True)


def _new_bias(ln_row, n_heads, n_new):
    qh = ln_row.shape[1]
    shift = n_heads.bit_length() - 1
    jrow = lax.broadcasted_iota(I32, (SUBLANES, qh), 0)
    ipos = lax.broadcasted_iota(I32, (SUBLANES, qh), 1) >> shift
    lmask = jnp.where(ipos <= jrow, jnp.broadcast_to(ln_row, (SUBLANES, qh)), 0.0)
    same = _head_match(qh, qh, n_heads).astype(F32)
    tcum = _dot_hi(lmask, same)
    cn = jnp.sum(jnp.where(jrow == ipos, tcum, 0.0), axis=0, keepdims=True)
    valid = (jrow <= ipos) & (jrow < n_new)
    return cn, jnp.where(valid, cn - tcum, -jnp.inf)


def _decode_cn_kernel(ln_ref, cn_ref, *, n_heads, n_new):
    cn, _ = _new_bias(ln_ref[...], n_heads, n_new)
    cn_ref[...] = cn


def _decode_pv_kernel(pt_ref, s_ref, v_ref, qall_ref, kn_ref, vn_ref, ln_ref, rep_ref, msk_ref, o_ref,
                      m_scr, linv_scr, pn_scr, acc, *, n_heads, n_new, scale):
    p = pl.program_id(1)
    n_pages = pl.num_programs(1)
    page = v_ref.shape[0]

    @pl.when(p == 0)
    def _():
        _, bias_new = _new_bias(ln_ref[...], n_heads, n_new)
        s_new = _page_scores(kn_ref[...], qall_ref[...], n_heads) * scale + bias_new
        s_all = s_ref[...]
        m = jnp.maximum(jnp.max(s_all, axis=0, keepdims=True), jnp.max(s_new, axis=0, keepdims=True))
        e_new = jnp.exp(s_new - m)
        l = jnp.sum(jnp.exp(s_all - m), axis=0, keepdims=True) + jnp.sum(e_new, axis=0, keepdims=True)
        linv = 1.0 / l
        m_scr[...] = m
        linv_scr[...] = linv
        pn_scr[...] = jnp.zeros_like(pn_scr)
        pn_scr[0:SUBLANES, :] = e_new * linv
        acc[...] = jnp.zeros_like(acc)

    start = pl.multiple_of(p * page, page)
    ps = jnp.exp(s_ref[pl.ds(start, page), :] - m_scr[...]) * linv_scr[...]
    acc[...] += _page_pv(ps, v_ref[...], rep_ref[...], msk_ref[...])

    @pl.when(p == n_pages - 1)
    def _():
        vn = vn_ref[...]
        vpage = jnp.concatenate([vn, jnp.zeros((page - vn.shape[0],) + vn.shape[1:], vn.dtype)], axis=0)
        o_ref[...] = acc[...] + _page_pv(pn_scr[...], vpage, rep_ref[...], msk_ref[...])


def _fox_decode(q_bf, k_new_bf, v_new_bf, logf_new, cache_k, cache_v, cache_logf, page_table):
    bd, n_pages = page_table.shape
    _, page, n_heads, dh = cache_k.shape
    n_new = q_bf.shape[0] // bd
    d = q_bf.shape[1]
    qh = n_new * n_heads
    past = n_pages * page
    scale = dh ** -0.5
    pt_flat = page_table.reshape(-1)
    qall = q_bf.reshape(bd, n_new, n_heads, dh).transpose(0, 3, 1, 2).reshape(bd, dh, qh)
    ln_row = logf_new.reshape(bd, 1, qh)
    pad = jnp.zeros((bd, SUBLANES - n_new, n_heads, dh), BF16)
    kn8 = jnp.concatenate([k_new_bf.reshape(bd, n_new, n_heads, dh), pad], axis=1)
    vn8 = jnp.concatenate([v_new_bf.reshape(bd, n_new, n_heads, dh), pad], axis=1)
    col = jnp.arange(page * n_heads, dtype=I32)
    rep = (jnp.arange(page, dtype=I32)[:, None] == col[None, :] // n_heads).astype(BF16)
    msk = (jnp.arange(qh, dtype=I32)[:, None] % n_heads == col[None, :] % n_heads).astype(BF16)

    cn = pl.pallas_call(
        functools.partial(_decode_cn_kernel, n_heads=n_heads, n_new=n_new), grid=(bd,),
        in_specs=[pl.BlockSpec((None, 1, qh), lambda b: (b, 0, 0))],
        out_specs=pl.BlockSpec((None, 1, qh), lambda b: (b, 0, 0)),
        out_shape=jax.ShapeDtypeStruct((bd, 1, qh), F32),
        compiler_params=_params(("arbitrary",)), name="fox_decode_newcum",
    )(ln_row)

    rev = lambda b, p, pt: (pt[b * n_pages + (n_pages - 1 - p)], 0, 0)
    rev4 = lambda b, p, pt: (pt[b * n_pages + (n_pages - 1 - p)], 0, 0, 0)
    scores = pl.pallas_call(
        functools.partial(_decode_scores_kernel, n_heads=n_heads, scale=scale),
        grid_spec=pltpu.PrefetchScalarGridSpec(
            num_scalar_prefetch=1, grid=(bd, n_pages),
            in_specs=[pl.BlockSpec((None, page, n_heads, dh), rev4),
                      pl.BlockSpec((None, page, n_heads), rev),
                      pl.BlockSpec((None, dh, qh), lambda b, p, pt: (b, 0, 0)),
                      pl.BlockSpec((None, 1, qh), lambda b, p, pt: (b, 0, 0))],
            out_specs=pl.BlockSpec((None, page, qh), lambda b, p, pt: (b, n_pages - 1 - p, 0)),
            scratch_shapes=[pltpu.VMEM((1, qh), F32)]),
        out_shape=jax.ShapeDtypeStruct((bd, past, qh), F32),
        compiler_params=_params(("arbitrary", "arbitrary")), name="fox_decode_scores",
    )(pt_flat, cache_k, cache_logf, qall, cn)

    fwd4 = lambda b, p, pt: (pt[b * n_pages + p], 0, 0, 0)
    per_b = lambda b, p, pt: (b, 0, 0)
    per_b4 = lambda b, p, pt: (b, 0, 0, 0)
    const = lambda b, p, pt: (0, 0)
    out = pl.pallas_call(
        functools.partial(_decode_pv_kernel, n_heads=n_heads, n_new=n_new, scale=scale),
        grid_spec=pltpu.PrefetchScalarGridSpec(
            num_scalar_prefetch=1, grid=(bd, n_pages),
            in_specs=[pl.BlockSpec((None, past, qh), per_b),
                      pl.BlockSpec((None, page, n_heads, dh), fwd4),
                      pl.BlockSpec((None, dh, qh), per_b),
                      pl.BlockSpec((None, SUBLANES, n_heads, dh), per_b4),
                      pl.BlockSpec((None, SUBLANES, n_heads, dh), per_b4),
                      pl.BlockSpec((None, 1, qh), per_b),
                      pl.BlockSpec((page, page * n_heads), const),
                      pl.BlockSpec((qh, page * n_heads), const)],
            out_specs=pl.BlockSpec((None, qh, dh), per_b),
            scratch_shapes=[pltpu.VMEM((1, qh), F32), pltpu.VMEM((1, qh), F32),
                            pltpu.VMEM((page, qh), F32), pltpu.VMEM((qh, dh), F32)]),
        out_shape=jax.ShapeDtypeStruct((bd, qh, dh), F32),
        compiler_params=_params(("arbitrary", "arbitrary")), name="fox_decode_pv",
    )(pt_flat, scores, cache_v, qall, kn8, vn8, ln_row, rep, msk)
    return out.reshape(bd * n_new, d)


def _gather_rows_kernel(idx_ref, nu_ref, src_ref, o_ref, buf, sem, *, tm):
    b = pl.program_id(0)
    n_used = nu_ref[0]

    def issue(blk, slot):
        def body(r8, carry):
            for k in range(SUBLANES):
                r = r8 * SUBLANES + k
                pltpu.make_async_copy(src_ref.at[pl.ds(idx_ref[blk * tm + r], 1)],
                                      buf.at[slot, pl.ds(r, 1)], sem.at[slot]).start()
            return carry
        lax.fori_loop(0, tm // SUBLANES, body, 0)

    @pl.when(b < n_used)
    def _():
        slot = b % 2

        @pl.when(b == 0)
        def _():
            issue(0, 0)

        @pl.when(b + 1 < n_used)
        def _():
            issue(b + 1, 1 - slot)
        pltpu.make_async_copy(src_ref.at[pl.ds(0, tm)], buf.at[slot], sem.at[slot]).wait()
        o_ref[...] = buf[slot].astype(BF16)

    @pl.when(b >= n_used)
    def _():
        o_ref[...] = jnp.zeros_like(o_ref)


def _gather_rows(src, slot_tok, n_used, tm):
    n_slots = slot_tok.shape[0]
    d = src.shape[1]
    return pl.pallas_call(
        functools.partial(_gather_rows_kernel, tm=tm),
        grid_spec=pltpu.PrefetchScalarGridSpec(
            num_scalar_prefetch=2, grid=(n_slots // tm,),
            in_specs=[pl.BlockSpec(memory_space=pl.ANY)],
            out_specs=pl.BlockSpec((tm, d), lambda b, idx, nu: (b, 0)),
            scratch_shapes=[pltpu.VMEM((2, tm, d), F32), pltpu.SemaphoreType.DMA((2,))]),
        out_shape=jax.ShapeDtypeStruct((n_slots, d), BF16),
        compiler_params=_params(("arbitrary",)), name="moe_gather",
    )(slot_tok, n_used, src)


def _stream_weights(i, ifirst, glen, nv, copies, cast):
    @pl.when(ifirst[i] == 1)
    def _():
        @pl.when(i == 0)
        def _():
            for c in copies(0):
                c.start()
        for c in copies(i):
            c.wait()
        cast()
        nxt = i + glen[i]

        @pl.when(nxt < nv[0])
        def _():
            for c in copies(nxt):
                c.start()


def _moe_up_kernel(ib, ic, ifirst, glen, be, nv, x_ref, w_hbm, o_ref, stg_g, stg_u, sg, su, sem,
                   *, layer, tf, fdim):
    i = pl.program_id(0)

    def copies(j):
        e = be[ib[j]]
        col = pl.multiple_of(ic[j] * tf, tf)
        return (pltpu.make_async_copy(w_hbm.at[layer, e, :, pl.ds(col, tf)], stg_g, sem.at[0]),
                pltpu.make_async_copy(w_hbm.at[layer, e, :, pl.ds(fdim + col, tf)], stg_u, sem.at[1]))

    def cast():
        sg[...] = stg_g[...].astype(BF16)
        su[...] = stg_u[...].astype(BF16)

    @pl.when(i < nv[0])
    def _():
        _stream_weights(i, ifirst, glen, nv, copies, cast)
        x = x_ref[...]
        g = _dot(x, sg[...])
        u = _dot(x, su[...])
        o_ref[...] = ((g / (1.0 + jnp.exp(-g))) * u).astype(BF16)

    @pl.when(i >= nv[0])
    def _():
        o_ref[...] = jnp.zeros_like(o_ref)


def _moe_down_kernel(ib, ic, ifirst, glen, be, nv, h_ref, w_hbm, o_ref, stg, sd, sem, *, layer, tn):
    i = pl.program_id(0)

    def copies(j):
        col = pl.multiple_of(ic[j] * tn, tn)
        return (pltpu.make_async_copy(w_hbm.at[layer, be[ib[j]], :, pl.ds(col, tn)], stg, sem.at[0]),)

    def cast():
        sd[...] = stg[...].astype(BF16)

    @pl.when(i < nv[0])
    def _():
        _stream_weights(i, ifirst, glen, nv, copies, cast)
        o_ref[...] = _dot(h_ref[...], sd[...])

    @pl.when(i >= nv[0])
    def _():
        o_ref[...] = jnp.zeros_like(o_ref)


def _work_list(block_expert, pad_start, padded, n_used, n_blocks, n_chunks, tm):
    b = jnp.arange(n_blocks, dtype=I32)
    valid = b < n_used
    bstart = (pad_start[block_expert] // tm).astype(I32)
    nb = (padded[block_expert] // tm).astype(I32)
    c = jnp.arange(n_chunks, dtype=I32)[None, :]
    pos_valid = n_chunks * bstart[:, None] + c * nb[:, None] + (b - bstart)[:, None]
    pos_tail = n_chunks * n_used + (b[:, None] - n_used) * n_chunks + c
    pos = jnp.where(valid[:, None], pos_valid, pos_tail).reshape(-1)
    shape = (n_blocks, n_chunks)
    vb = jnp.broadcast_to(b[:, None], shape).reshape(-1)
    vc = jnp.broadcast_to(c, shape).reshape(-1)
    vf = jnp.broadcast_to((valid & (b == bstart)).astype(I32)[:, None], shape).reshape(-1)
    vn = jnp.broadcast_to(nb[:, None], shape).reshape(-1)
    n_items = n_blocks * n_chunks
    assert n_blocks < 4096 and n_chunks < 16
    packed = vb | (vc << 12) | (vf << 16) | (vn << 17)
    items = jnp.zeros((n_items,), I32).at[pos].set(packed)
    return (items & 4095, (items >> 12) & 15, (items >> 16) & 1, items >> 17,
            (n_chunks * n_used).reshape(1).astype(I32))


def _moe_experts(xs, w_gu, w_down, layer, disp, tm):
    n_slots, d = xs.shape
    fdim = w_down.shape[2]
    n_blocks = n_slots // tm
    block_expert = disp["block_expert"]
    tf = _tile(fdim, 512)
    n_f = fdim // tf
    ib, ic, ifirst, glen, nv = _work_list(block_expert, disp["pad_start"], disp["padded"], disp["n_used"][0],
                                          n_blocks, n_f, tm)
    hmid = pl.pallas_call(
        functools.partial(_moe_up_kernel, layer=layer, tf=tf, fdim=fdim),
        grid_spec=pltpu.PrefetchScalarGridSpec(
            num_scalar_prefetch=6, grid=(n_blocks * n_f,),
            in_specs=[pl.BlockSpec((tm, d), lambda i, ib, ic, fs, gl, be, nv: (ib[i], 0)),
                      pl.BlockSpec(memory_space=pl.ANY)],
            out_specs=pl.BlockSpec((tm, tf), lambda i, ib, ic, fs, gl, be, nv: (ib[i], ic[i])),
            scratch_shapes=[pltpu.VMEM((d, tf), F32), pltpu.VMEM((d, tf), F32),
                            pltpu.VMEM((d, tf), BF16), pltpu.VMEM((d, tf), BF16),
                            pltpu.SemaphoreType.DMA((2,))]),
        out_shape=jax.ShapeDtypeStruct((n_slots, fdim), BF16),
        compiler_params=_params(("arbitrary",)), name="moe_up",
    )(ib, ic, ifirst, glen, block_expert, nv, xs, w_gu)

    tn = d
    n_n = d // tn
    ib, ic, ifirst, glen, nv = _work_list(block_expert, disp["pad_start"], disp["padded"], disp["n_used"][0],
                                          n_blocks, n_n, tm)
    return pl.pallas_call(
        functools.partial(_moe_down_kernel, layer=layer, tn=tn),
        grid_spec=pltpu.PrefetchScalarGridSpec(
            num_scalar_prefetch=6, grid=(n_blocks * n_n,),
            in_specs=[pl.BlockSpec((tm, fdim), lambda i, ib, ic, fs, gl, be, nv: (ib[i], 0)),
                      pl.BlockSpec(memory_space=pl.ANY)],
            out_specs=pl.BlockSpec((tm, tn), lambda i, ib, ic, fs, gl, be, nv: (ib[i], ic[i])),
            scratch_shapes=[pltpu.VMEM((fdim, tn), F32), pltpu.VMEM((fdim, tn), BF16),
                            pltpu.SemaphoreType.DMA((1,))]),
        out_shape=jax.ShapeDtypeStruct((n_slots, d), F32),
        compiler_params=_params(("arbitrary",)), name="moe_down",
    )(ib, ic, ifirst, glen, block_expert, nv, hmid, w_down)


def _dispatch(experts, n_experts, tm):
    t = experts.shape[1]
    n_assign = TOP_K * t
    n_blocks = -(-n_assign // tm) + n_experts
    n_slots = n_blocks * tm
    e_flat = experts.reshape(n_assign)
    onehot = (e_flat[:, None] == jnp.arange(n_experts, dtype=I32)[None, :]).astype(I32)
    rank = jnp.sum((jnp.cumsum(onehot, axis=0) - onehot) * onehot, axis=1)
    counts = jnp.sum(onehot, axis=0)
    padded = (counts + tm - 1) // tm * tm
    pad_end = jnp.cumsum(padded)
    pad_start = pad_end - padded
    dest = (pad_start[e_flat] + rank).astype(I32)
    tok = jnp.arange(n_assign, dtype=I32) % t
    slot_tok = jnp.zeros((n_slots,), I32).at[dest].set(tok)
    n_used = (pad_end[-1] // tm).astype(I32)
    blocks = jnp.arange(n_blocks, dtype=I32)
    block_expert = jnp.minimum(jnp.searchsorted(pad_end, jnp.minimum(blocks, n_used - 1) * tm, side="right"),
                               n_experts - 1).astype(I32)
    return dict(slot_tok=slot_tok, dest=dest, block_expert=block_expert, n_used=n_used.reshape(1),
                pad_start=pad_start.astype(I32), padded=padded.astype(I32))


def kernel(x_prompt, x_sample, state_conv, cache_k, cache_v, cache_logf, page_table, p_prompt, p_sample,
           conv_w_in, conv_w_dw, conv_w_out, attn_w_q, attn_w_o, kv_w, forget_w, forget_b,
           router_g_w, router_g_b, router_e_w, router_e_b, moe_w_gu, moe_w_down,
           ln_g, ln_b, ple_w_proj, ple_w_gate):
    bsz, seq, d = x_prompt.shape
    bd, dseq, _ = x_sample.shape
    depth = ln_g.shape[0]
    n_a = conv_w_in.shape[0]
    n_heads, dh = cache_k.shape[2], cache_k.shape[3]
    n_groups = router_g_w.shape[-1]
    n_experts = router_e_w.shape[-1]
    per_group = n_experts // n_groups
    alpha = (2 * depth) ** 0.25
    tp, ts = bsz * seq, bd * dseq
    t_all = tp + ts
    moe_tm = 256
    assert n_heads & (n_heads - 1) == 0 and dh & (dh - 1) == 0 and n_groups <= SUBLANES
    assert tp % ts == 0 and per_group == SUBLANES

    ln_g4 = ln_g.reshape(depth, 3, 1, d)
    ln_b4 = ln_b.reshape(depth, 3, 1, d)
    zpad = jnp.zeros((depth, SUBLANES - n_groups, d), F32)
    wr_t = jnp.concatenate([router_g_w.transpose(0, 2, 1), zpad, router_e_w.transpose(0, 2, 1)], axis=1)
    br = jnp.concatenate([router_g_b, jnp.zeros((depth, SUBLANES - n_groups), F32), router_e_b],
                         axis=1)[:, :, None]
    groups = [
        dict(x=x_prompt.reshape(tp, d), p=p_prompt.reshape(depth, tp, -1), n=tp, seq=seq, nseq=bsz,
             prev=None, conv=[]),
        dict(x=x_sample.reshape(ts, d), p=p_sample.reshape(depth, ts, -1), n=ts, seq=dseq, nseq=bd,
             prev=state_conv, conv=[]),
    ]
    for g in groups:
        g["xb"] = g["x"].astype(BF16)

    for i in range(depth):
        for gi, g in enumerate(groups):
            if i < n_a:
                bgate, u = _mm_convin(g["xb"], conv_w_in, i, d)
                prev = jnp.zeros((g["nseq"], 2, d), F32) if g["prev"] is None else g["prev"][i]
                gated = _short_conv(bgate, u, prev, conv_w_dw, i, g["seq"], BF16 if gi == 0 else F32)
                g["conv"].append(u.reshape(g["nseq"], g["seq"], d)[:, g["seq"] - 2:, :])
                g["mix"] = _mm(gated.astype(BF16), conv_w_out, (i,), 0, [F32], "conv_out_proj", d)[0]
            else:
                if i == n_a:
                    g["k"], g["kb"] = _mm(g["xb"], kv_w, (), 0, [F32, BF16], "k_proj", d)
                    g["v"], g["vb"] = _mm(g["xb"], kv_w, (), d, [F32, BF16], "v_proj", d)
                    g["logf"] = _forget(g["xb"], forget_w, forget_b)
                    if gi == 0:
                        g["c"], g["ct"] = _logf_cumsum(g["logf"], bsz, seq)
                j = i - n_a
                q = _mm(g["xb"], attn_w_q, (j,), 0, [BF16], "q_proj", d)[0]
                if gi == 0:
                    att = _fox_prompt(q, g["kb"], g["vb"], g["c"], g["ct"], bsz, seq, n_heads, dh)
                else:
                    att = _fox_decode(q, g["kb"], g["vb"], g["logf"], cache_k, cache_v, cache_logf,
                                      page_table).astype(BF16)
                g["mix"] = _mm(att, attn_w_o, (j,), 0, [F32], "attn_out_proj", d)[0]
        hs, es, gs = [], [], []
        for g in groups:
            h, e, gate = _ln_router(g["x"], g["mix"], ln_g4, ln_b4, wr_t, br, i, alpha, n_groups, per_group)
            hs.append(h)
            es.append(e)
            gs.append(gate)
        h_all = jnp.concatenate(hs, axis=0)
        experts = jnp.concatenate(es, axis=1)
        gates_t = jnp.concatenate(gs, axis=1).T
        disp = _dispatch(experts, n_experts, moe_tm)
        xs = _gather_rows(h_all, disp["slot_tok"], disp["n_used"], moe_tm)
        ys = _moe_experts(xs, moe_w_gu, moe_w_down, i, disp, moe_tm)
        row_off = 0
        for g in groups:
            h2, h2b = _ln_moe(h_all, ys, disp["dest"], gates_t, ln_g4, ln_b4, i, alpha, g["n"], row_off)
            row_off += g["n"]
            ple = _mm_ple(h2b, ple_w_gate, g["p"][i], ple_w_proj, i, d)
            g["x"], g["xb"] = _ln(h2, ple, ln_g4, ln_b4, i, 2, alpha, g["n"])

    gp, gs_ = groups
    return (gp["x"].reshape(bsz, seq, d), gs_["x"].reshape(bd, dseq, d),
            jnp.stack(gp["conv"]),
            _heads_layout(gp["k"], n_heads, dh).reshape(bsz, seq, n_heads, dh),
            _heads_layout(gp["v"], n_heads, dh).reshape(bsz, seq, n_heads, dh),
            gp["logf"].reshape(bsz, seq, n_heads),
            jnp.stack(gs_["conv"]),
            _heads_layout(gs_["k"], n_heads, dh).reshape(bd, dseq, n_heads, dh),
            _heads_layout(gs_["v"], n_heads, dh).reshape(bd, dseq, n_heads, dh),
            gs_["logf"].reshape(bd, dseq, n_heads))
```

```python
import functools

import jax
import jax.numpy as jnp
from jax import lax
from jax.experimental import pallas as pl
from jax.experimental.pallas import tpu as pltpu

F32 = jnp.float32
BF16 = jnp.bfloat16
I32 = jnp.int32
HIGHEST = lax.Precision.HIGHEST

LN_EPS = 1e-5
LOG2E = 1.4426950408889634
TOP_K = 2
V7X_VMEM_LIMIT_BYTES = 56 * 1024 * 1024
SUBLANES = 8

_NT = (((1,), (1,)), ((), ()))


def _params(semantics, vmem=V7X_VMEM_LIMIT_BYTES):
    return pltpu.CompilerParams(dimension_semantics=semantics, vmem_limit_bytes=vmem)


def _tile(n, pref):
    t = min(n, pref)
    while n % t:
        t //= 2
    return t


def _dot(a, b):
    return jnp.dot(a, b, preferred_element_type=F32)


def _dot_hi(a, b):
    return jnp.dot(a, b, preferred_element_type=F32, precision=HIGHEST)


def _cast_weights(w_refs, wbf_refs):
    @pl.when(pl.program_id(1) == 0)
    def _():
        for w, wb in zip(w_refs, wbf_refs):
            wb[...] = w[...].astype(BF16)


def _mm_plain_kernel(x_ref, w_ref, *rest, n_out):
    o_refs, wbf = rest[:n_out], rest[n_out]
    _cast_weights([w_ref], [wbf])
    acc = _dot(x_ref[...], wbf[...])
    for o in o_refs:
        o[...] = acc.astype(o.dtype)


def _mm_convin_kernel(x_ref, wb_ref, wc_ref, wx_ref, b_out, u_out, sb, sc, sx):
    _cast_weights([wb_ref, wc_ref, wx_ref], [sb, sc, sx])
    x = x_ref[...]
    b_out[...] = _dot(x, sb[...])
    u_out[...] = _dot(x, sc[...]) * _dot(x, sx[...])


def _mm_ple_kernel(x_ref, w_ref, p_ref, wp_ref, o_ref, wbf):
    _cast_weights([w_ref], [wbf])
    gate = _dot(x_ref[...], wbf[...])
    proj = _dot(p_ref[...].astype(BF16), wp_ref[...].astype(BF16))
    o_ref[...] = proj * (1.0 / (1.0 + jnp.exp(-gate)))


def _w_spec(w, lead, k, tn, col_off):
    none = (None,) * len(lead)
    return pl.BlockSpec(none + (k, tn), lambda n, m: lead + (0, col_off + n))


def _dense(kern, x, weights, out_dtypes, *, n_cols, tm, tn, extra=(), extra_specs=(), name):
    m_rows, k = x.shape
    tm = _tile(m_rows, tm)
    tn = _tile(n_cols, tn)
    grid = (n_cols // tn, m_rows // tm)
    in_specs = [pl.BlockSpec((tm, k), lambda n, m: (m, 0))]
    args = [x]
    for w, lead, off in weights:
        in_specs.append(_w_spec(w, lead, k, tn, off // tn))
        args.append(w)
    in_specs += list(extra_specs)
    args += list(extra)
    out_shape = [jax.ShapeDtypeStruct((m_rows, n_cols), dt) for dt in out_dtypes]
    out_specs = [pl.BlockSpec((tm, tn), lambda n, m: (m, n)) for _ in out_dtypes]
    scratch = [pltpu.VMEM((k, tn), BF16) for _ in weights]
    return pl.pallas_call(
        kern, grid=grid, in_specs=in_specs, out_specs=out_specs, out_shape=out_shape,
        scratch_shapes=scratch, compiler_params=_params(("arbitrary", "arbitrary")), name=name,
    )(*args)


def _mm(x, w, lead, col_off, out_dtypes, name, n_cols):
    kern = functools.partial(_mm_plain_kernel, n_out=len(out_dtypes))
    return _dense(kern, x, [(w, lead, col_off)], out_dtypes, n_cols=n_cols, tm=1024, tn=512, name=name)


def _mm_convin(x, w_in, layer, d):
    ws = [(w_in, (layer,), j * d) for j in range(3)]
    return _dense(_mm_convin_kernel, x, ws, [F32, F32], n_cols=d, tm=512, tn=256, name="conv_in_proj")


def _mm_ple(h_bf, w_gate, p, w_proj, layer, d):
    m_rows = h_bf.shape[0]
    tm, tn = _tile(m_rows, 1024), _tile(d, 512)
    pd = p.shape[-1]
    extra_specs = [pl.BlockSpec((tm, pd), lambda n, m: (m, 0)),
                   pl.BlockSpec((None, pd, tn), lambda n, m: (layer, 0, n))]
    return _dense(_mm_ple_kernel, h_bf, [(w_gate, (layer,), 0)], [F32], n_cols=d, tm=tm, tn=tn,
                  extra=(p, w_proj), extra_specs=extra_specs, name="ple_gate_proj")[0]


def _forget_kernel(x_ref, w_ref, b_ref, o_ref):
    z = _dot(x_ref[...], w_ref[...].astype(BF16)) + b_ref[...]
    o_ref[...] = -(jnp.maximum(-z, 0.0) + jnp.log1p(jnp.exp(-jnp.abs(z))))


def _forget(x_bf, forget_w, forget_b):
    m_rows, k = x_bf.shape
    h = forget_w.shape[1]
    tm = _tile(m_rows, 1024)
    return pl.pallas_call(
        _forget_kernel, grid=(m_rows // tm,),
        in_specs=[pl.BlockSpec((tm, k), lambda m: (m, 0)),
                  pl.BlockSpec((k, h), lambda m: (0, 0)),
                  pl.BlockSpec((1, h), lambda m: (0, 0))],
        out_specs=pl.BlockSpec((tm, h), lambda m: (m, 0)),
        out_shape=jax.ShapeDtypeStruct((m_rows, h), F32),
        compiler_params=_params(("arbitrary",)), name="forget_proj",
    )(x_bf, forget_w, forget_b.reshape(1, h))


def _heads_layout_kernel(x_ref, o_ref, *, n_heads, dh):
    for h in range(n_heads):
        o_ref[:, h, :] = x_ref[:, h * dh:(h + 1) * dh]


def _heads_layout(x, n_heads, dh):
    t, d = x.shape
    tm = _tile(t, 256)
    return pl.pallas_call(
        functools.partial(_heads_layout_kernel, n_heads=n_heads, dh=dh), grid=(t // tm,),
        in_specs=[pl.BlockSpec((tm, d), lambda i: (i, 0))],
        out_specs=pl.BlockSpec((tm, n_heads, dh), lambda i: (i, 0, 0)),
        out_shape=jax.ShapeDtypeStruct((t, n_heads, dh), x.dtype),
        compiler_params=_params(("arbitrary",)), name="heads_layout",
    )(x)


def _ln_rows(v, g, b):
    mu = jnp.mean(v, axis=-1, keepdims=True)
    xc = v - mu
    var = jnp.mean(xc * xc, axis=-1, keepdims=True)
    return xc * lax.rsqrt(var + LN_EPS) * g + b


def _ln_kernel(x_ref, mix_ref, g_ref, b_ref, of_ref, ob_ref, *, alpha):
    h = _ln_rows(alpha * x_ref[...] + mix_ref[...], g_ref[...], b_ref[...])
    of_ref[...] = h
    ob_ref[...] = h.astype(BF16)


def _ln_moe_kernel(dest_ref, x_ref, ys_ref, gt_ref, g_ref, b_ref, of_ref, ob_ref, buf, sem,
                   *, alpha, tm, row_off, t_all):
    i = pl.program_id(0)
    slot = i % 2

    def issue(step, into):
        base = row_off + step * tm

        def body(r4, carry):
            for j in range(4):
                r = r4 * 4 + j
                for k in range(TOP_K):
                    pltpu.make_async_copy(ys_ref.at[pl.ds(dest_ref[k * t_all + base + r], 1)],
                                          buf.at[into, k, pl.ds(r, 1)], sem.at[into]).start()
            return carry
        lax.fori_loop(0, tm // 4, body, 0)

    @pl.when(i == 0)
    def _():
        issue(0, 0)

    @pl.when(i + 1 < pl.num_programs(0))
    def _():
        issue(i + 1, 1 - slot)
    for k in range(TOP_K):
        pltpu.make_async_copy(ys_ref.at[pl.ds(0, tm)], buf.at[slot, k], sem.at[slot]).wait()
    gt = gt_ref[...]
    moe = buf[slot, 0] * gt[:, 0:1] + buf[slot, 1] * gt[:, 1:2]
    h = _ln_rows(alpha * x_ref[...] + moe, g_ref[...], b_ref[...])
    of_ref[...] = h
    ob_ref[...] = h.astype(BF16)


def _ln_router_kernel(xp_ref, mp_ref, xs_ref, ms_ref, g_ref, b_ref, wr_ref, br_ref, of_ref, e_ref, gate_ref,
                      *, alpha, n_groups, per_group, n_prompt_tiles):
    i = pl.program_id(0)
    route = functools.partial(_ln_route, g_ref=g_ref, b_ref=b_ref, wr_ref=wr_ref, br_ref=br_ref,
                              of_ref=of_ref, e_ref=e_ref, gate_ref=gate_ref, alpha=alpha,
                              n_groups=n_groups, per_group=per_group)

    @pl.when(i < n_prompt_tiles)
    def _():
        route(xp_ref[...], mp_ref[...])

    @pl.when(i == n_prompt_tiles)
    def _():
        of_ref[...] = jnp.zeros_like(of_ref)
        e_ref[...] = jnp.zeros_like(e_ref)
        gate_ref[...] = jnp.zeros_like(gate_ref)
        route(xs_ref[...], ms_ref[...])


def _ln_route(x, mix, *, g_ref, b_ref, wr_ref, br_ref, of_ref, e_ref, gate_ref, alpha, n_groups, per_group):
    tm = x.shape[0]
    h = _ln_rows(alpha * x + mix, g_ref[...], b_ref[...])
    of_ref[0:tm, :] = h
    logits = lax.dot_general(wr_ref[...], h, _NT, precision=HIGHEST,
                             preferred_element_type=F32) + br_ref[...]
    lg = logits[0:n_groups]
    gmax = jnp.max(lg, axis=0, keepdims=True)
    gi = lax.broadcasted_iota(I32, lg.shape, 0)
    grp = jnp.min(jnp.where(lg == gmax, gi, n_groups), axis=0, keepdims=True)
    p_grp = 1.0 / jnp.sum(jnp.exp(lg - gmax), axis=0, keepdims=True)
    sel = logits[SUBLANES + (n_groups - 1) * per_group:SUBLANES + n_groups * per_group]
    for g in range(n_groups - 2, -1, -1):
        sel = jnp.where(grp == g, logits[SUBLANES + g * per_group:SUBLANES + (g + 1) * per_group], sel)
    ex = jnp.exp(sel - jnp.max(sel, axis=0, keepdims=True))
    ei = lax.broadcasted_iota(I32, (per_group, tm), 0)
    p1 = jnp.max(ex, axis=0, keepdims=True)
    i1 = jnp.min(jnp.where(ex == p1, ei, per_group), axis=0, keepdims=True)
    ex2 = jnp.where(ei == i1, -1.0, ex)
    p2 = jnp.max(ex2, axis=0, keepdims=True)
    i2 = jnp.min(jnp.where(ex2 == p2, ei, per_group), axis=0, keepdims=True)
    scale = p_grp / (p1 + p2)
    e_ref[0:1, 0:tm] = grp * per_group + i1
    e_ref[1:2, 0:tm] = grp * per_group + i2
    gate_ref[0:1, 0:tm] = p1 * scale
    gate_ref[1:2, 0:tm] = p2 * scale


def _row_spec(tm, width, row_off_blocks=0):
    return pl.BlockSpec((tm, width), lambda i: (row_off_blocks + i, 0))


def _vec_spec(d, layer, which):
    return pl.BlockSpec((None, None, 1, d), lambda i: (layer, which, 0, 0))


def _ln(x, mix, ln_g4, ln_b4, layer, which, alpha, n_rows, x_row_off=0, tm_pref=256):
    d = mix.shape[1]
    tm = _tile(n_rows, tm_pref)
    return pl.pallas_call(
        functools.partial(_ln_kernel, alpha=alpha), grid=(n_rows // tm,),
        in_specs=[_row_spec(tm, d, x_row_off // tm), _row_spec(tm, d),
                  _vec_spec(d, layer, which), _vec_spec(d, layer, which)],
        out_specs=[_row_spec(tm, d), _row_spec(tm, d)],
        out_shape=[jax.ShapeDtypeStruct((n_rows, d), F32), jax.ShapeDtypeStruct((n_rows, d), BF16)],
        compiler_params=_params(("arbitrary",)), name="ln_residual",
    )(x, mix, ln_g4, ln_b4)


def _ln_moe(h_all, ys, dest, gates_t, ln_g4, ln_b4, layer, alpha, n_rows, row_off, tm_pref=256):
    d = h_all.shape[1]
    t_all = dest.shape[0] // TOP_K
    tm = _tile(n_rows, tm_pref)
    off = row_off // tm
    rows = lambda w: pl.BlockSpec((tm, w), lambda i, dst: (off + i, 0))
    vec = pl.BlockSpec((None, None, 1, d), lambda i, dst: (layer, 1, 0, 0))
    out = pl.BlockSpec((tm, d), lambda i, dst: (i, 0))
    return pl.pallas_call(
        functools.partial(_ln_moe_kernel, alpha=alpha, tm=tm, row_off=row_off, t_all=t_all),
        grid_spec=pltpu.PrefetchScalarGridSpec(
            num_scalar_prefetch=1, grid=(n_rows // tm,),
            in_specs=[rows(d), pl.BlockSpec(memory_space=pl.ANY), rows(TOP_K), vec, vec],
            out_specs=[out, out],
            scratch_shapes=[pltpu.VMEM((2, TOP_K, tm, d), F32), pltpu.SemaphoreType.DMA((2,))]),
        out_shape=[jax.ShapeDtypeStruct((n_rows, d), F32), jax.ShapeDtypeStruct((n_rows, d), BF16)],
        compiler_params=_params(("arbitrary",)), name="ln_moe_combine",
    )(dest, h_all, ys, gates_t, ln_g4, ln_b4)


def _ln_router(xp, mp, xs, ms, ln_g4, ln_b4, wr_t, br, layer, alpha, n_groups, per_group, tm_pref=256):
    tp, d = xp.shape
    ts = xs.shape[0]
    tm = _tile(tp, tm_pref)
    assert ts <= tm
    n_tiles = tp // tm
    rows_out = tp + tm
    r = wr_t.shape[1]
    prompt = pl.BlockSpec((tm, d), lambda i: (jnp.minimum(i, n_tiles - 1), 0))
    sample = pl.BlockSpec((ts, d), lambda i: (0, 0))
    return pl.pallas_call(
        functools.partial(_ln_router_kernel, alpha=alpha, n_groups=n_groups, per_group=per_group,
                          n_prompt_tiles=n_tiles),
        grid=(n_tiles + 1,),
        in_specs=[prompt, prompt, sample, sample, _vec_spec(d, layer, 0), _vec_spec(d, layer, 0),
                  pl.BlockSpec((None, r, d), lambda i: (layer, 0, 0)),
                  pl.BlockSpec((None, r, 1), lambda i: (layer, 0, 0))],
        out_specs=[_row_spec(tm, d), pl.BlockSpec((TOP_K, tm), lambda i: (0, i)),
                   pl.BlockSpec((TOP_K, tm), lambda i: (0, i))],
        out_shape=[jax.ShapeDtypeStruct((rows_out, d), F32), jax.ShapeDtypeStruct((TOP_K, rows_out), I32),
                   jax.ShapeDtypeStruct((TOP_K, rows_out), F32)],
        compiler_params=_params(("arbitrary",)), name="ln_router",
    )(xp, mp, xs, ms, ln_g4, ln_b4, wr_t, br)


def _conv_kernel(b_ref, u_ref, halo_ref, w_ref, o_ref, scr, *, ts):
    base = SUBLANES
    scr[base - 2:base, :] = halo_ref[...]
    scr[base:base + ts, :] = u_ref[...]
    w = w_ref[...]
    conv = (w[0:1] * scr[base - 2:base - 2 + ts, :] + w[1:2] * scr[base - 1:base - 1 + ts, :]
            + w[2:3] * scr[base:base + ts, :])
    o_ref[...] = (b_ref[...] * conv).astype(o_ref.dtype)


def _short_conv(b, u, prev, w_dw, layer, seq, out_dtype):
    t, d = u.shape
    ts = _tile(seq, 256)
    n_tiles = t // ts
    per_seq = seq // ts
    u3 = u.reshape(n_tiles, ts, d)
    if per_seq > 1:
        tails = u3[:, ts - 2:, :]
        shifted = jnp.concatenate([jnp.zeros((1, 2, d), F32), tails[:-1]], axis=0)
        first = (jnp.arange(n_tiles) % per_seq == 0)[:, None, None]
        halo = jnp.where(first, jnp.repeat(prev, per_seq, axis=0), shifted)
    else:
        halo = prev
    blk = pl.BlockSpec((None, ts, d), lambda i: (i, 0, 0))
    out = pl.pallas_call(
        functools.partial(_conv_kernel, ts=ts), grid=(n_tiles,),
        in_specs=[blk, blk, pl.BlockSpec((None, 2, d), lambda i: (i, 0, 0)),
                  pl.BlockSpec((None, 3, d), lambda i: (layer, 0, 0))],
        out_specs=blk, out_shape=jax.ShapeDtypeStruct((n_tiles, ts, d), out_dtype),
        scratch_shapes=[pltpu.VMEM((ts + SUBLANES, d), F32)],
        compiler_params=_params(("arbitrary",)), name="short_conv",
    )(b.reshape(n_tiles, ts, d), u3, halo, w_dw)
    return out.reshape(t, d)


def _cumsum_kernel(lf_ref, lft_ref, c_ref, ct_ref, *, blk):
    s, h = lf_ref.shape
    r = lax.broadcasted_iota(I32, (blk, blk), 0)
    c = lax.broadcasted_iota(I32, (blk, blk), 1)
    lower = (c <= r).astype(F32)
    upper = (r <= c).astype(F32)
    carry = jnp.zeros((1, h), F32)
    carry_t = jnp.zeros((h, 1), F32)
    for j in range(s // blk):
        x = lf_ref[j * blk:(j + 1) * blk, :]
        c_ref[j * blk:(j + 1) * blk, :] = _dot_hi(lower, x) + carry
        carry = carry + jnp.sum(x, axis=0, keepdims=True)
        xt = lft_ref[:, j * blk:(j + 1) * blk]
        ct_ref[:, j * blk:(j + 1) * blk] = _dot_hi(xt, upper) + carry_t
        carry_t = carry_t + jnp.sum(xt, axis=1, keepdims=True)


def _logf_cumsum(logf, bsz, seq):
    h = logf.shape[1]
    lf = logf.reshape(bsz, seq, h)
    lft = lf.transpose(0, 2, 1)
    blk = _tile(seq, 256)
    c, ct = pl.pallas_call(
        functools.partial(_cumsum_kernel, blk=blk), grid=(bsz,),
        in_specs=[pl.BlockSpec((None, seq, h), lambda b: (b, 0, 0)),
                  pl.BlockSpec((None, h, seq), lambda b: (b, 0, 0))],
        out_specs=[pl.BlockSpec((None, seq, h), lambda b: (b, 0, 0)),
                   pl.BlockSpec((None, h, seq), lambda b: (b, 0, 0))],
        out_shape=[jax.ShapeDtypeStruct((bsz, seq, h), F32), jax.ShapeDtypeStruct((bsz, h, seq), F32)],
        compiler_params=_params(("arbitrary",)), name="logf_cumsum",
    )(lf, lft)
    return c, ct.reshape(bsz, h, 1, seq)


def _flash_kernel(q_ref, k_ref, v_ref, c_ref, ct_ref, o_ref, *, tq, tk, scale):
    hd = pl.program_id(1)
    qi = pl.program_id(2)
    q = q_ref[...]
    cblk = c_ref[...]
    lane = lax.broadcasted_iota(I32, cblk.shape, 1)
    cq2 = jnp.sum(jnp.where(lane == hd, cblk, 0.0), axis=1, keepdims=True) * LOG2E
    scale2 = scale * LOG2E

    def tile(j, carry, diagonal):
        m, l, acc = carry
        start = pl.multiple_of(j * tk, tk)
        kj = k_ref[pl.ds(start, tk), :]
        vj = v_ref[pl.ds(start, tk), :]
        ck2 = ct_ref[:, pl.ds(start, tk)] * LOG2E
        s = lax.dot_general(q, kj, _NT, preferred_element_type=F32) * scale2 + (cq2 - ck2)
        if diagonal:
            row = lax.broadcasted_iota(I32, (tq, tk), 0)
            col = lax.broadcasted_iota(I32, (tq, tk), 1)
            s = jnp.where(col <= row, s, -jnp.inf)
        m_new = jnp.maximum(m, jnp.max(s, axis=1, keepdims=True))
        a = jnp.exp2(m - m_new)
        p = jnp.exp2(s - m_new)
        l = a * l + jnp.sum(p, axis=1, keepdims=True)
        acc = a * acc + _dot(p.astype(BF16), vj)
        return m_new, l, acc

    init = (jnp.full((tq, 1), -jnp.inf, F32), jnp.zeros((tq, 1), F32), jnp.zeros((tq, q.shape[1]), F32))
    carry = lax.fori_loop(0, qi, lambda j, c: tile(j, c, False), init)
    m, l, acc = tile(qi, carry, True)
    o_ref[...] = (acc / l).astype(o_ref.dtype)


def _fox_prompt(q, k, v, c, ct, bsz, seq, n_heads, dh):
    t, d = q.shape
    tq = _tile(seq, 512)
    tk = tq
    nq = seq // tq
    return pl.pallas_call(
        functools.partial(_flash_kernel, tq=tq, tk=tk, scale=dh ** -0.5),
        grid=(bsz, n_heads, nq),
        in_specs=[pl.BlockSpec((tq, dh), lambda b, h, i: (b * nq + i, h)),
                  pl.BlockSpec((seq, dh), lambda b, h, i: (b, h)),
                  pl.BlockSpec((seq, dh), lambda b, h, i: (b, h)),
                  pl.BlockSpec((None, tq, n_heads), lambda b, h, i: (b, i, 0)),
                  pl.BlockSpec((None, None, 1, seq), lambda b, h, i: (b, h, 0, 0))],
        out_specs=pl.BlockSpec((tq, dh), lambda b, h, i: (b * nq + i, h)),
        out_shape=jax.ShapeDtypeStruct((t, d), BF16),
        compiler_params=_params(("arbitrary", "arbitrary", "arbitrary")), name="fox_prompt",
    )(q, k, v, c, ct)


def _head_match(rows, cols, n_heads):
    r = lax.broadcasted_iota(I32, (rows, cols), 0) & (n_heads - 1)
    c = lax.broadcasted_iota(I32, (rows, cols), 1) & (n_heads - 1)
    return r == c


def _page_scores(k3, qall, n_heads):
    p, h, dh = k3.shape
    qh = qall.shape[1]
    s2 = _dot(k3.reshape(p * h, dh).astype(BF16), qall)
    own = _head_match(h, qh, n_heads)
    return jnp.sum(jnp.where(own[None], s2.reshape(p, h, qh), 0.0), axis=1)


def _page_pv(ps, v3, rep, msk):
    p, h, dh = v3.shape
    pexp = _dot(ps.T.astype(BF16), rep).astype(BF16) * msk
    return _dot(pexp, v3.reshape(p * h, dh).astype(BF16))


def _decode_scores_kernel(pt_ref, *refs, n_heads, scale, pp):
    k_refs, lf_refs = refs[:pp], refs[pp:2 * pp]
    qall_ref, cn_ref, s_ref, carry = refs[2 * pp:]
    p = pl.program_id(1)
    page, h = lf_refs[0].shape
    qh = qall_ref.shape[1]

    @pl.when(p == 0)
    def _():
        carry[...] = cn_ref[...]

    expand = _head_match(h, qh, n_heads).astype(F32)
    r = lax.broadcasted_iota(I32, (page, page), 0)
    c = lax.broadcasted_iota(I32, (page, page), 1)
    later = (c > r).astype(F32)
    qall = qall_ref[...]
    run = carry[...]
    for j in range(pp):
        lft = _dot_hi(lf_refs[j][...], expand)
        bias = _dot_hi(later, lft) + run
        rows = (pp - 1 - j) * page
        s_ref[rows:rows + page, :] = _page_scores(k_refs[j][...], qall, n_heads) * scale + bias
        run = run + jnp.sum(lft, axis=0, keepdims=True)
    carry[...] = run


def _new_bias(ln_row, n_heads, n_new):
    qh = ln_row.shape[1]
    shift = n_heads.bit_length() - 1
    jrow = lax.broadcasted_iota(I32, (SUBLANES, qh), 0)
    ipos = lax.broadcasted_iota(I32, (SUBLANES, qh), 1) >> shift
    lmask = jnp.where(ipos <= jrow, jnp.broadcast_to(ln_row, (SUBLANES, qh)), 0.0)
    same = _head_match(qh, qh, n_heads).astype(F32)
    tcum = _dot_hi(lmask, same)
    cn = jnp.sum(jnp.where(jrow == ipos, tcum, 0.0), axis=0, keepdims=True)
    valid = (jrow <= ipos) & (jrow < n_new)
    return cn, jnp.where(valid, cn - tcum, -jnp.inf)


def _decode_cn_kernel(ln_ref, cn_ref, *, n_heads, n_new):
    cn, _ = _new_bias(ln_ref[...], n_heads, n_new)
    cn_ref[...] = cn


def _decode_pv_kernel(pt_ref, s_ref, *refs, n_heads, n_new, scale, pp):
    v_refs = refs[:pp]
    (qall_ref, kn_ref, vn_ref, ln_ref, rep_ref, msk_ref, o_ref, m_scr, linv_scr, pn_scr, acc) = refs[pp:]
    p = pl.program_id(1)
    n_steps = pl.num_programs(1)
    page = v_refs[0].shape[0]

    @pl.when(p == 0)
    def _():
        _, bias_new = _new_bias(ln_ref[...], n_heads, n_new)
        s_new = _page_scores(kn_ref[...], qall_ref[...], n_heads) * scale + bias_new
        s_all = s_ref[...]
        m = jnp.maximum(jnp.max(s_all, axis=0, keepdims=True), jnp.max(s_new, axis=0, keepdims=True))
        e_new = jnp.exp(s_new - m)
        l = jnp.sum(jnp.exp(s_all - m), axis=0, keepdims=True) + jnp.sum(e_new, axis=0, keepdims=True)
        linv = 1.0 / l
        m_scr[...] = m
        linv_scr[...] = linv
        pn_scr[...] = jnp.zeros_like(pn_scr)
        pn_scr[0:SUBLANES, :] = e_new * linv
        acc[...] = jnp.zeros_like(acc)

    part = None
    for j in range(pp):
        start = pl.multiple_of((p * pp + j) * page, page)
        ps = jnp.exp(s_ref[pl.ds(start, page), :] - m_scr[...]) * linv_scr[...]
        pv = _page_pv(ps, v_refs[j][...], rep_ref[...], msk_ref[...])
        part = pv if part is None else part + pv
    acc[...] += part

    @pl.when(p == n_steps - 1)
    def _():
        vn = vn_ref[...]
        vpage = jnp.concatenate([vn, jnp.zeros((page - vn.shape[0],) + vn.shape[1:], vn.dtype)], axis=0)
        o_ref[...] = acc[...] + _page_pv(pn_scr[...], vpage, rep_ref[...], msk_ref[...])


def _fox_decode(q_bf, k_new_bf, v_new_bf, logf_new, cache_k, cache_v, cache_logf, page_table):
    bd, n_pages = page_table.shape
    _, page, n_heads, dh = cache_k.shape
    n_new = q_bf.shape[0] // bd
    d = q_bf.shape[1]
    qh = n_new * n_heads
    past = n_pages * page
    scale = dh ** -0.5
    pt_flat = page_table.reshape(-1)
    qall = q_bf.reshape(bd, n_new, n_heads, dh).transpose(0, 3, 1, 2).reshape(bd, dh, qh)
    ln_row = logf_new.reshape(bd, 1, qh)
    pad = jnp.zeros((bd, SUBLANES - n_new, n_heads, dh), BF16)
    kn8 = jnp.concatenate([k_new_bf.reshape(bd, n_new, n_heads, dh), pad], axis=1)
    vn8 = jnp.concatenate([v_new_bf.reshape(bd, n_new, n_heads, dh), pad], axis=1)
    col = jnp.arange(page * n_heads, dtype=I32)
    rep = (jnp.arange(page, dtype=I32)[:, None] == col[None, :] // n_heads).astype(BF16)
    msk = (jnp.arange(qh, dtype=I32)[:, None] % n_heads == col[None, :] % n_heads).astype(BF16)

    cn = pl.pallas_call(
        functools.partial(_decode_cn_kernel, n_heads=n_heads, n_new=n_new), grid=(bd,),
        in_specs=[pl.BlockSpec((None, 1, qh), lambda b: (b, 0, 0))],
        out_specs=pl.BlockSpec((None, 1, qh), lambda b: (b, 0, 0)),
        out_shape=jax.ShapeDtypeStruct((bd, 1, qh), F32),
        compiler_params=_params(("arbitrary",)), name="fox_decode_newcum",
    )(ln_row)

    pp = _tile(n_pages, 4)
    n_steps = n_pages // pp

    def rev(j, rank):
        return lambda b, p, pt: (pt[b * n_pages + (n_pages - 1 - (p * pp + j))],) + (0,) * (rank - 1)

    def fwd(j, rank):
        return lambda b, p, pt: (pt[b * n_pages + p * pp + j],) + (0,) * (rank - 1)

    scores = pl.pallas_call(
        functools.partial(_decode_scores_kernel, n_heads=n_heads, scale=scale, pp=pp),
        grid_spec=pltpu.PrefetchScalarGridSpec(
            num_scalar_prefetch=1, grid=(bd, n_steps),
            in_specs=([pl.BlockSpec((None, page, n_heads, dh), rev(j, 4)) for j in range(pp)]
                      + [pl.BlockSpec((None, page, n_heads), rev(j, 3)) for j in range(pp)]
                      + [pl.BlockSpec((None, dh, qh), lambda b, p, pt: (b, 0, 0)),
                         pl.BlockSpec((None, 1, qh), lambda b, p, pt: (b, 0, 0))]),
            out_specs=pl.BlockSpec((None, pp * page, qh), lambda b, p, pt: (b, n_steps - 1 - p, 0)),
            scratch_shapes=[pltpu.VMEM((1, qh), F32)]),
        out_shape=jax.ShapeDtypeStruct((bd, past, qh), F32),
        compiler_params=_params(("arbitrary", "arbitrary")), name="fox_decode_scores",
    )(pt_flat, *([cache_k] * pp), *([cache_logf] * pp), qall, cn)

    per_b = lambda b, p, pt: (b, 0, 0)
    per_b4 = lambda b, p, pt: (b, 0, 0, 0)
    const = lambda b, p, pt: (0, 0)
    out = pl.pallas_call(
        functools.partial(_decode_pv_kernel, n_heads=n_heads, n_new=n_new, scale=scale, pp=pp),
        grid_spec=pltpu.PrefetchScalarGridSpec(
            num_scalar_prefetch=1, grid=(bd, n_steps),
            in_specs=[pl.BlockSpec((None, past, qh), per_b)]
                     + [pl.BlockSpec((None, page, n_heads, dh), fwd(j, 4)) for j in range(pp)]
                     + [pl.BlockSpec((None, dh, qh), per_b),
                      pl.BlockSpec((None, SUBLANES, n_heads, dh), per_b4),
                      pl.BlockSpec((None, SUBLANES, n_heads, dh), per_b4),
                      pl.BlockSpec((None, 1, qh), per_b),
                      pl.BlockSpec((page, page * n_heads), const),
                      pl.BlockSpec((qh, page * n_heads), const)],
            out_specs=pl.BlockSpec((None, qh, dh), per_b),
            scratch_shapes=[pltpu.VMEM((1, qh), F32), pltpu.VMEM((1, qh), F32),
                            pltpu.VMEM((page, qh), F32), pltpu.VMEM((qh, dh), F32)]),
        out_shape=jax.ShapeDtypeStruct((bd, qh, dh), F32),
        compiler_params=_params(("arbitrary", "arbitrary")), name="fox_decode_pv",
    )(pt_flat, scores, *([cache_v] * pp), qall, kn8, vn8, ln_row, rep, msk)
    return out.reshape(bd * n_new, d)


def _gather_rows_kernel(idx_ref, nu_ref, nrow_ref, src_ref, o_ref, buf, sem, *, tm):
    b = pl.program_id(0)
    n_used = nu_ref[0]

    def groups(blk):
        return (nrow_ref[blk] + SUBLANES - 1) // SUBLANES

    def issue(blk, slot):
        def body(r8, carry):
            for k in range(SUBLANES):
                r = r8 * SUBLANES + k
                pltpu.make_async_copy(src_ref.at[pl.ds(idx_ref[blk * tm + r], 1)],
                                      buf.at[slot, pl.ds(r, 1)], sem.at[slot]).start()
            return carry
        lax.fori_loop(0, groups(blk), body, 0)

    @pl.when(b < n_used)
    def _():
        slot = b % 2

        @pl.when(b == 0)
        def _():
            buf[...] = jnp.zeros_like(buf)
            issue(0, 0)

        @pl.when(b + 1 < n_used)
        def _():
            issue(b + 1, 1 - slot)

        def wait_group(r8, carry):
            pltpu.make_async_copy(src_ref.at[pl.ds(0, SUBLANES)], buf.at[slot, pl.ds(0, SUBLANES)],
                                  sem.at[slot]).wait()
            return carry
        lax.fori_loop(0, groups(b), wait_group, 0)
        o_ref[...] = buf[slot].astype(BF16)

    @pl.when(b >= n_used)
    def _():
        o_ref[...] = jnp.zeros_like(o_ref)


def _gather_rows(src, slot_tok, n_used, block_rows, tm):
    n_slots = slot_tok.shape[0]
    d = src.shape[1]
    return pl.pallas_call(
        functools.partial(_gather_rows_kernel, tm=tm),
        grid_spec=pltpu.PrefetchScalarGridSpec(
            num_scalar_prefetch=3, grid=(n_slots // tm,),
            in_specs=[pl.BlockSpec(memory_space=pl.ANY)],
            out_specs=pl.BlockSpec((tm, d), lambda b, idx, nu, nr: (b, 0)),
            scratch_shapes=[pltpu.VMEM((2, tm, d), F32), pltpu.SemaphoreType.DMA((2,))]),
        out_shape=jax.ShapeDtypeStruct((n_slots, d), BF16),
        compiler_params=_params(("arbitrary",)), name="moe_gather",
    )(slot_tok, n_used, block_rows, src)


def _stream_weights(i, ifirst, glen, nv, copies, cast):
    @pl.when(ifirst[i] == 1)
    def _():
        @pl.when(i == 0)
        def _():
            for c in copies(0):
                c.start()
        for c in copies(i):
            c.wait()
        cast()
        nxt = i + glen[i]

        @pl.when(nxt < nv[0])
        def _():
            for c in copies(nxt):
                c.start()


def _moe_up_kernel(ib, ic, ifirst, glen, be, nv, x_ref, w_hbm, o_ref, stg_g, stg_u, sg, su, sem,
                   *, layer, tf, fdim):
    i = pl.program_id(0)

    def copies(j):
        e = be[ib[j]]
        col = pl.multiple_of(ic[j] * tf, tf)
        return (pltpu.make_async_copy(w_hbm.at[layer, e, :, pl.ds(col, tf)], stg_g, sem.at[0]),
                pltpu.make_async_copy(w_hbm.at[layer, e, :, pl.ds(fdim + col, tf)], stg_u, sem.at[1]))

    def cast():
        sg[...] = stg_g[...].astype(BF16)
        su[...] = stg_u[...].astype(BF16)

    @pl.when(i < nv[0])
    def _():
        _stream_weights(i, ifirst, glen, nv, copies, cast)
        x = x_ref[...]
        g = _dot(x, sg[...])
        u = _dot(x, su[...])
        o_ref[...] = ((g / (1.0 + jnp.exp(-g))) * u).astype(BF16)

    @pl.when(i >= nv[0])
    def _():
        o_ref[...] = jnp.zeros_like(o_ref)


def _moe_down_kernel(ib, ic, ifirst, glen, be, nv, h_ref, w_hbm, o_ref, stg, sd, sem, *, layer, tn):
    i = pl.program_id(0)

    def copies(j):
        col = pl.multiple_of(ic[j] * tn, tn)
        return (pltpu.make_async_copy(w_hbm.at[layer, be[ib[j]], :, pl.ds(col, tn)], stg, sem.at[0]),)

    def cast():
        sd[...] = stg[...].astype(BF16)

    @pl.when(i < nv[0])
    def _():
        _stream_weights(i, ifirst, glen, nv, copies, cast)
        o_ref[...] = _dot(h_ref[...], sd[...])

    @pl.when(i >= nv[0])
    def _():
        o_ref[...] = jnp.zeros_like(o_ref)


def _work_list(block_expert, pad_start, padded, n_used, n_blocks, n_chunks, tm):
    b = jnp.arange(n_blocks, dtype=I32)
    valid = b < n_used
    bstart = (pad_start[block_expert] // tm).astype(I32)
    nb = (padded[block_expert] // tm).astype(I32)
    c = jnp.arange(n_chunks, dtype=I32)[None, :]
    pos_valid = n_chunks * bstart[:, None] + c * nb[:, None] + (b - bstart)[:, None]
    pos_tail = n_chunks * n_used + (b[:, None] - n_used) * n_chunks + c
    pos = jnp.where(valid[:, None], pos_valid, pos_tail).reshape(-1)
    shape = (n_blocks, n_chunks)
    vb = jnp.broadcast_to(b[:, None], shape).reshape(-1)
    vc = jnp.broadcast_to(c, shape).reshape(-1)
    vf = jnp.broadcast_to((valid & (b == bstart)).astype(I32)[:, None], shape).reshape(-1)
    vn = jnp.broadcast_to(nb[:, None], shape).reshape(-1)
    n_items = n_blocks * n_chunks
    assert n_blocks < 4096 and n_chunks < 16
    packed = vb | (vc << 12) | (vf << 16) | (vn << 17)
    items = jnp.zeros((n_items,), I32).at[pos].set(packed)
    return (items & 4095, (items >> 12) & 15, (items >> 16) & 1, items >> 17,
            (n_chunks * n_used).reshape(1).astype(I32))


def _moe_experts(xs, w_gu, w_down, layer, disp, tm):
    n_slots, d = xs.shape
    fdim = w_down.shape[2]
    n_blocks = n_slots // tm
    block_expert = disp["block_expert"]
    tf = _tile(fdim, 512)
    n_f = fdim // tf
    ib, ic, ifirst, glen, nv = _work_list(block_expert, disp["pad_start"], disp["padded"], disp["n_used"][0],
                                          n_blocks, n_f, tm)
    hmid = pl.pallas_call(
        functools.partial(_moe_up_kernel, layer=layer, tf=tf, fdim=fdim),
        grid_spec=pltpu.PrefetchScalarGridSpec(
            num_scalar_prefetch=6, grid=(n_blocks * n_f,),
            in_specs=[pl.BlockSpec((tm, d), lambda i, ib, ic, fs, gl, be, nv: (ib[i], 0)),
                      pl.BlockSpec(memory_space=pl.ANY)],
            out_specs=pl.BlockSpec((tm, tf), lambda i, ib, ic, fs, gl, be, nv: (ib[i], ic[i])),
            scratch_shapes=[pltpu.VMEM((d, tf), F32), pltpu.VMEM((d, tf), F32),
                            pltpu.VMEM((d, tf), BF16), pltpu.VMEM((d, tf), BF16),
                            pltpu.SemaphoreType.DMA((2,))]),
        out_shape=jax.ShapeDtypeStruct((n_slots, fdim), BF16),
        compiler_params=_params(("arbitrary",)), name="moe_up",
    )(ib, ic, ifirst, glen, block_expert, nv, xs, w_gu)

    tn = d
    n_n = d // tn
    ib, ic, ifirst, glen, nv = _work_list(block_expert, disp["pad_start"], disp["padded"], disp["n_used"][0],
                                          n_blocks, n_n, tm)
    return pl.pallas_call(
        functools.partial(_moe_down_kernel, layer=layer, tn=tn),
        grid_spec=pltpu.PrefetchScalarGridSpec(
            num_scalar_prefetch=6, grid=(n_blocks * n_n,),
            in_specs=[pl.BlockSpec((tm, fdim), lambda i, ib, ic, fs, gl, be, nv: (ib[i], 0)),
                      pl.BlockSpec(memory_space=pl.ANY)],
            out_specs=pl.BlockSpec((tm, tn), lambda i, ib, ic, fs, gl, be, nv: (ib[i], ic[i])),
            scratch_shapes=[pltpu.VMEM((fdim, tn), F32), pltpu.VMEM((fdim, tn), BF16),
                            pltpu.SemaphoreType.DMA((1,))]),
        out_shape=jax.ShapeDtypeStruct((n_slots, d), F32),
        compiler_params=_params(("arbitrary",)), name="moe_down",
    )(ib, ic, ifirst, glen, block_expert, nv, hmid, w_down)


def _dispatch(experts, n_experts, tm):
    t = experts.shape[1]
    n_assign = TOP_K * t
    n_blocks = -(-n_assign // tm) + n_experts
    n_slots = n_blocks * tm
    e_flat = experts.reshape(n_assign)
    onehot = (e_flat[:, None] == jnp.arange(n_experts, dtype=I32)[None, :]).astype(I32)
    rank = jnp.sum((jnp.cumsum(onehot, axis=0) - onehot) * onehot, axis=1)
    counts = jnp.sum(onehot, axis=0)
    padded = (counts + tm - 1) // tm * tm
    pad_end = jnp.cumsum(padded)
    pad_start = pad_end - padded
    dest = (pad_start[e_flat] + rank).astype(I32)
    tok = jnp.arange(n_assign, dtype=I32) % t
    slot_tok = jnp.zeros((n_slots,), I32).at[dest].set(tok)
    n_used = (pad_end[-1] // tm).astype(I32)
    blocks = jnp.arange(n_blocks, dtype=I32)
    block_expert = jnp.minimum(jnp.searchsorted(pad_end, jnp.minimum(blocks, n_used - 1) * tm, side="right"),
                               n_experts - 1).astype(I32)
    first_block = pad_start[block_expert] // tm
    block_rows = jnp.clip(counts[block_expert] - (blocks - first_block) * tm, 0, tm)
    block_rows = jnp.where(blocks < n_used, block_rows, 0).astype(I32)
    return dict(slot_tok=slot_tok, dest=dest, block_expert=block_expert, n_used=n_used.reshape(1),
                pad_start=pad_start.astype(I32), padded=padded.astype(I32), block_rows=block_rows)


def kernel(x_prompt, x_sample, state_conv, cache_k, cache_v, cache_logf, page_table, p_prompt, p_sample,
           conv_w_in, conv_w_dw, conv_w_out, attn_w_q, attn_w_o, kv_w, forget_w, forget_b,
           router_g_w, router_g_b, router_e_w, router_e_b, moe_w_gu, moe_w_down,
           ln_g, ln_b, ple_w_proj, ple_w_gate):
    bsz, seq, d = x_prompt.shape
    bd, dseq, _ = x_sample.shape
    depth = ln_g.shape[0]
    n_a = conv_w_in.shape[0]
    n_heads, dh = cache_k.shape[2], cache_k.shape[3]
    n_groups = router_g_w.shape[-1]
    n_experts = router_e_w.shape[-1]
    per_group = n_experts // n_groups
    alpha = (2 * depth) ** 0.25
    tp, ts = bsz * seq, bd * dseq
    t_all = tp + ts
    moe_tm = 256
    assert n_heads & (n_heads - 1) == 0 and dh & (dh - 1) == 0 and n_groups <= SUBLANES
    assert tp % ts == 0 and per_group == SUBLANES

    ln_g4 = ln_g.reshape(depth, 3, 1, d)
    ln_b4 = ln_b.reshape(depth, 3, 1, d)
    zpad = jnp.zeros((depth, SUBLANES - n_groups, d), F32)
    wr_t = jnp.concatenate([router_g_w.transpose(0, 2, 1), zpad, router_e_w.transpose(0, 2, 1)], axis=1)
    br = jnp.concatenate([router_g_b, jnp.zeros((depth, SUBLANES - n_groups), F32), router_e_b],
                         axis=1)[:, :, None]
    groups = [
        dict(x=x_prompt.reshape(tp, d), p=p_prompt.reshape(depth, tp, -1), n=tp, seq=seq, nseq=bsz,
             prev=None, conv=[]),
        dict(x=x_sample.reshape(ts, d), p=p_sample.reshape(depth, ts, -1), n=ts, seq=dseq, nseq=bd,
             prev=state_conv, conv=[]),
    ]
    for g in groups:
        g["xb"] = g["x"].astype(BF16)

    for i in range(depth):
        for gi, g in enumerate(groups):
            if i < n_a:
                bgate, u = _mm_convin(g["xb"], conv_w_in, i, d)
                prev = jnp.zeros((g["nseq"], 2, d), F32) if g["prev"] is None else g["prev"][i]
                gated = _short_conv(bgate, u, prev, conv_w_dw, i, g["seq"], BF16 if gi == 0 else F32)
                g["conv"].append(u.reshape(g["nseq"], g["seq"], d)[:, g["seq"] - 2:, :])
                g["mix"] = _mm(gated.astype(BF16), conv_w_out, (i,), 0, [F32], "conv_out_proj", d)[0]
            else:
                if i == n_a:
                    g["k"], g["kb"] = _mm(g["xb"], kv_w, (), 0, [F32, BF16], "k_proj", d)
                    g["v"], g["vb"] = _mm(g["xb"], kv_w, (), d, [F32, BF16], "v_proj", d)
                    g["logf"] = _forget(g["xb"], forget_w, forget_b)
                    if gi == 0:
                        g["c"], g["ct"] = _logf_cumsum(g["logf"], bsz, seq)
                j = i - n_a
                q = _mm(g["xb"], attn_w_q, (j,), 0, [BF16], "q_proj", d)[0]
                if gi == 0:
                    att = _fox_prompt(q, g["kb"], g["vb"], g["c"], g["ct"], bsz, seq, n_heads, dh)
                else:
                    att = _fox_decode(q, g["kb"], g["vb"], g["logf"], cache_k, cache_v, cache_logf,
                                      page_table).astype(BF16)
                g["mix"] = _mm(att, attn_w_o, (j,), 0, [F32], "attn_out_proj", d)[0]
        gp_, gs_ = groups
        h_all, experts, gates = _ln_router(gp_["x"], gp_["mix"], gs_["x"], gs_["mix"], ln_g4, ln_b4, wr_t, br,
                                           i, alpha, n_groups, per_group)
        experts = experts[:, :t_all]
        gates_t = gates[:, :t_all].T
        disp = _dispatch(experts, n_experts, moe_tm)
        xs = _gather_rows(h_all, disp["slot_tok"], disp["n_used"], disp["block_rows"], moe_tm)
        ys = _moe_experts(xs, moe_w_gu, moe_w_down, i, disp, moe_tm)
        row_off = 0
        for g in groups:
            h2, h2b = _ln_moe(h_all, ys, disp["dest"], gates_t, ln_g4, ln_b4, i, alpha, g["n"], row_off)
            row_off += g["n"]
            ple = _mm_ple(h2b, ple_w_gate, g["p"][i], ple_w_proj, i, d)
            g["x"], g["xb"] = _ln(h2, ple, ln_g4, ln_b4, i, 2, alpha, g["n"])

    gp, gs_ = groups
    return (gp["x"].reshape(bsz, seq, d), gs_["x"].reshape(bd, dseq, d),
            jnp.stack(gp["conv"]),
            _heads_layout(gp["k"], n_heads, dh).reshape(bsz, seq, n_heads, dh),
            _heads_layout(gp["v"], n_heads, dh).reshape(bsz, seq, n_heads, dh),
            gp["logf"].reshape(bsz, seq, n_heads),
            jnp.stack(gs_["conv"]),
            _heads_layout(gs_["k"], n_heads, dh).reshape(bd, dseq, n_heads, dh),
            _heads_layout(gs_["v"], n_heads, dh).reshape(bd, dseq, n_heads, dh),
            gs_["logf"].reshape(bd, dseq, n_heads))
```
